```python
import jax, jax.numpy as jnp
from jax import lax
import numpy as np

D_MODEL = 2048
BATCH = 8
SEQ = 2048
DEPTH = 1

PLE_DIM = 256
BLK = 128
EPS = 1e-6

A_HEAD_DIM = 128
A_WIDTH = D_MODEL // 2
A_HEADS = A_WIDTH // A_HEAD_DIM
A_BRANCHES = ((128, 1), (512, 4), (2048, 16))

B_HEAD_DIM = 64
B_WIDTH = D_MODEL - A_WIDTH
B_HEADS = B_WIDTH // B_HEAD_DIM
B_GROUP = 8
B_KV_HEADS = B_HEADS // B_GROUP
B_WINDOW = 128

MIX_WIDTH = A_WIDTH + B_WIDTH
IN_COLS = 3 * A_WIDTH + B_WIDTH + 2 * B_KV_HEADS * B_HEAD_DIM

N_GROUPS = 4
EXPERTS_PER_GROUP = 8
N_EXPERTS = N_GROUPS * EXPERTS_PER_GROUP
TOP_K = 2
D_EXPERT = D_MODEL // 8

kernel_name = "hymba_dilated_sinkswa_hiermoe_ple"


def rms_norm(x, g):
    xf = x.astype(jnp.float32)
    y = xf * lax.rsqrt(jnp.mean(xf * xf, axis=-1, keepdims=True) + EPS)
    return (y * g.astype(jnp.float32)).astype(x.dtype)


def alibi_slopes(n):
    return jnp.asarray(np.array([2.0 ** (-8.0 * (i + 1) / n) for i in range(n)], dtype=np.float32))


def _with_prev_block(xb):
    prev = jnp.pad(xb[:, :, :-1], ((0, 0), (0, 0), (1, 0), (0, 0), (0, 0)))
    return jnp.concatenate([prev, xb], axis=3)


def banded_attention(q, k, v, slopes, max_delta, stride, sinks=None):
    n, hkv, g, L, hd = q.shape
    nb = L // BLK
    qb = q.reshape(n, hkv, g, nb, BLK, hd)
    kb = _with_prev_block(k.reshape(n, hkv, nb, BLK, hd))
    vb = _with_prev_block(v.reshape(n, hkv, nb, BLK, hd))
    s = jnp.einsum('nkgbqd,nkbsd->nkgbqs', qb, kb,
                   preferred_element_type=jnp.float32) * (hd ** -0.5)
    qi = jnp.arange(L).reshape(nb, BLK)
    ki = qi[:, :1] - BLK + jnp.arange(2 * BLK)[None, :]
    delta = qi[:, :, None] - ki[:, None, :]
    valid = (delta >= 0) & (delta <= max_delta) & (ki[:, None, :] >= 0)
    s = s - slopes.astype(jnp.float32)[:, :, None, None, None] * (stride * delta).astype(jnp.float32)
    s = jnp.where(valid, s, -jnp.inf)
    m = jnp.max(s, axis=-1)
    if sinks is not None:
        sk = sinks.astype(jnp.float32)[:, :, None, None]
        m = jnp.maximum(m, sk)
    pr = jnp.exp(s - m[..., None])
    den = jnp.sum(pr, axis=-1)
    if sinks is not None:
        den = den + jnp.exp(sk - m)
    o = jnp.einsum('nkgbqs,nkbsd->nkgbqd', pr.astype(v.dtype), vb,
                   preferred_element_type=jnp.float32) / den[..., None]
    lse = m + jnp.log(den)
    return o.reshape(n, hkv, g, L, hd).astype(q.dtype), lse.reshape(n, hkv, g, L)


def dilated_branch(q, k, v, slopes, window, dilation):
    b, h, s, hd = q.shape
    span = dilation * BLK
    sp = -(-s // span) * span
    L = sp // dilation

    def to_residues(t):
        t = jnp.pad(t, ((0, 0), (0, 0), (0, sp - s), (0, 0)))
        t = t.reshape(b, h, L, dilation, hd).transpose(0, 3, 1, 2, 4)
        return t.reshape(b * dilation, h, L, hd)

    qr, kr, vr = to_residues(q), to_residues(k), to_residues(v)
    o, lse = banded_attention(qr[:, :, None], kr, vr, slopes[:, None], window // dilation, dilation)
    o = o[:, :, 0].reshape(b, dilation, h, L, hd).transpose(0, 2, 3, 1, 4).reshape(b, h, sp, hd)[:, :, :s]
    lse = lse[:, :, 0].reshape(b, dilation, h, L).transpose(0, 2, 3, 1).reshape(b, h, sp)[:, :, :s]
    return o, lse


def dilated_attention(q, k, v, slopes):
    outs, lses = [], []
    for window, dilation in A_BRANCHES:
        o, lse = dilated_branch(q, k, v, slopes, window, dilation)
        outs.append(o)
        lses.append(lse)
    w = jax.nn.softmax(jnp.stack(lses), axis=0)
    return jnp.einsum('rbhs,rbhsd->bhsd', w.astype(q.dtype), jnp.stack(outs))


def sink_window_attention(q, k, v, slopes, sinks):
    b, hq, s, hd = q.shape
    qg = q.reshape(b, B_KV_HEADS, B_GROUP, s, hd)
    o, _ = banded_attention(qg, k, v, slopes.reshape(B_KV_HEADS, B_GROUP), B_WINDOW - 1, 1,
                            sinks.reshape(B_KV_HEADS, B_GROUP))
    return o.reshape(b, hq, s, hd)


def hier_moe(x, w_grp, b_grp, w_exp, b_exp, w_gate, w_up, w_down):
    t = x.shape[0]
    grp_logits = jnp.dot(x, w_grp, preferred_element_type=jnp.float32) + b_grp.astype(jnp.float32)
    grp_prob = jax.nn.softmax(grp_logits, axis=-1)
    grp_idx = jnp.argmax(grp_logits, axis=-1)
    grp_w = jnp.take_along_axis(grp_prob, grp_idx[:, None], axis=-1)
    exp_logits = jnp.einsum('td,dge->tge', x, w_exp,
                            preferred_element_type=jnp.float32) + b_exp.astype(jnp.float32)
    in_grp = jnp.take_along_axis(exp_logits, grp_idx[:, None, None], axis=1)[:, 0]
    top_val, top_idx = lax.top_k(in_grp, TOP_K)
    top_w = jax.nn.softmax(top_val, axis=-1) * grp_w
    eid = grp_idx[:, None] * EXPERTS_PER_GROUP + top_idx
    combine = jnp.sum(jax.nn.one_hot(eid, N_EXPERTS, dtype=jnp.float32) * top_w[..., None], axis=1)
    hg = jnp.einsum('td,edf->tef', x, w_gate)
    hu = jnp.einsum('td,edf->tef', x, w_up)
    hid = jax.nn.silu(hg) * hu * combine.astype(x.dtype)[..., None]
    return jnp.einsum('tef,efd->td', hid, w_down)


def setup_inputs(seed: int = 0) -> dict:
    key = jax.random.key(seed)
    ks = jax.random.split(key, 20)

    def nrm(k, shape, scale):
        return jax.random.normal(k, shape, jnp.float32) * scale

    return {
        "x": nrm(ks[0], (BATCH, SEQ, D_MODEL), 1.0),
        "p": nrm(ks[1], (DEPTH, BATCH, SEQ, PLE_DIM), 1.0),
        "w_in": nrm(ks[2], (DEPTH, D_MODEL, IN_COLS), D_MODEL ** -0.5),
        "w_out": nrm(ks[3], (DEPTH, MIX_WIDTH, D_MODEL), MIX_WIDTH ** -0.5),
        "sinks": nrm(ks[4], (DEPTH, B_HEADS), 1.0),
        "g_mix": 1.0 + nrm(ks[5], (DEPTH, D_MODEL), 0.02),
        "g_moe": 1.0 + nrm(ks[6], (DEPTH, D_MODEL), 0.02),
        "g_ple": 1.0 + nrm(ks[7], (DEPTH, D_MODEL), 0.02),
        "g_final": 1.0 + nrm(ks[8], (D_MODEL,), 0.02),
        "w_grp": nrm(ks[9], (DEPTH, D_MODEL, N_GROUPS), D_MODEL ** -0.5),
        "b_grp": nrm(ks[10], (DEPTH, N_GROUPS), 0.01),
        "w_exp": nrm(ks[11], (DEPTH, D_MODEL, N_GROUPS, EXPERTS_PER_GROUP), D_MODEL ** -0.5),
        "b_exp": nrm(ks[12], (DEPTH, N_GROUPS, EXPERTS_PER_GROUP), 0.01),
        "w_gate": nrm(ks[13], (DEPTH, N_EXPERTS, D_MODEL, D_EXPERT), D_MODEL ** -0.5),
        "w_up": nrm(ks[14], (DEPTH, N_EXPERTS, D_MODEL, D_EXPERT), D_MODEL ** -0.5),
        "w_down": nrm(ks[15], (DEPTH, N_EXPERTS, D_EXPERT, D_MODEL), D_EXPERT ** -0.5),
        "w_ple": nrm(ks[16], (DEPTH, PLE_DIM, D_MODEL), PLE_DIM ** -0.5),
        "w_ple_gate": nrm(ks[17], (DEPTH, D_MODEL, D_MODEL), D_MODEL ** -0.5),
    }


def reference(x, p, w_in, w_out, sinks, g_mix, g_moe, g_ple, g_final, w_grp, b_grp,
              w_exp, b_exp, w_gate, w_up, w_down, w_ple, w_ple_gate):
    b, s, d = x.shape
    slopes_a = alibi_slopes(A_HEADS)
    slopes_b = alibi_slopes(B_HEADS)
    kv_w = B_KV_HEADS * B_HEAD_DIM
    splits = [A_WIDTH, 2 * A_WIDTH, 3 * A_WIDTH, 3 * A_WIDTH + B_WIDTH, 3 * A_WIDTH + B_WIDTH + kv_w]

    def heads(t, n, hd):
        return t.reshape(b, s, n, hd).transpose(0, 2, 1, 3)

    h = x
    for i in range(DEPTH):
        a = rms_norm(h, g_mix[i])
        proj = a @ w_in[i]
        qa, ka, va, qb, kb, vb = jnp.split(proj, splits, axis=-1)
        oa = dilated_attention(heads(qa, A_HEADS, A_HEAD_DIM), heads(ka, A_HEADS, A_HEAD_DIM),
                               heads(va, A_HEADS, A_HEAD_DIM), slopes_a)
        ob = sink_window_attention(heads(qb, B_HEADS, B_HEAD_DIM), heads(kb, B_KV_HEADS, B_HEAD_DIM),
                                   heads(vb, B_KV_HEADS, B_HEAD_DIM), slopes_b, sinks[i])
        mixed = jnp.concatenate([oa.transpose(0, 2, 1, 3).reshape(b, s, A_WIDTH),
                                 ob.transpose(0, 2, 1, 3).reshape(b, s, B_WIDTH)], axis=-1)
        h = h + mixed @ w_out[i]
        m = rms_norm(h, g_moe[i]).reshape(b * s, d)
        h = h + hier_moe(m, w_grp[i], b_grp[i], w_exp[i], b_exp[i],
                         w_gate[i], w_up[i], w_down[i]).reshape(b, s, d)
        n = rms_norm(h, g_ple[i])
        h = h + jax.nn.sigmoid(n @ w_ple_gate[i]) * (p[i] @ w_ple[i])
    return rms_norm(h, g_final)
```

```python
import functools

import numpy as np
import jax
import jax.numpy as jnp
from jax import lax
from jax.experimental import pallas as pl
from jax.experimental.pallas import tpu as pltpu

D_MODEL = 2048
PLE_DIM = 256
BLK = 128
EPS = 1e-6
A_HEAD_DIM = 128
A_WIDTH = D_MODEL // 2
A_HEADS = A_WIDTH // A_HEAD_DIM
A_BRANCHES = ((128, 1), (512, 4), (2048, 16))
B_HEAD_DIM = 64
B_WIDTH = D_MODEL - A_WIDTH
B_HEADS = B_WIDTH // B_HEAD_DIM
B_GROUP = 8
B_KV_HEADS = B_HEADS // B_GROUP
B_WINDOW = 128
IN_COLS = 3 * A_WIDTH + B_WIDTH + 2 * B_KV_HEADS * B_HEAD_DIM
N_GROUPS = 4
EXPERTS_PER_GROUP = 8
N_EXPERTS = N_GROUPS * EXPERTS_PER_GROUP
TOP_K = 2
D_EXPERT = D_MODEL // 8

LANES = 128
V7X_VMEM_LIMIT_BYTES = 56 * 1024 * 1024

ROW_TILE = 512
EXPERT_TILE = 256

F32 = jnp.float32
BF16 = jnp.bfloat16
NEG_INF = float("-inf")


def _cparams(*sem):
    return pltpu.CompilerParams(dimension_semantics=sem, vmem_limit_bytes=V7X_VMEM_LIMIT_BYTES)


def _rms(x, g):
    return x * lax.rsqrt(jnp.mean(x * x, axis=-1, keepdims=True) + EPS) * g


def _alibi_slopes(n):
    return jnp.asarray(np.array([2.0 ** (-8.0 * (i + 1) / n) for i in range(n)], dtype=np.float32))


def _proj_kernel(x_ref, g_ref, w_ref, o_ref):
    a = _rms(x_ref[...], g_ref[...]).astype(BF16)
    o_ref[...] = jnp.dot(a, w_ref[...], preferred_element_type=F32)


def _proj(x2, g, w_bf16):
    t, d = x2.shape
    n = w_bf16.shape[1]
    return pl.pallas_call(
        _proj_kernel,
        out_shape=jax.ShapeDtypeStruct((t, n), F32),
        grid=(t // ROW_TILE,),
        in_specs=[
            pl.BlockSpec((ROW_TILE, d), lambda i: (i, 0)),
            pl.BlockSpec((1, d), lambda i: (0, 0)),
            pl.BlockSpec((d, n), lambda i: (0, 0), pipeline_mode=pl.Buffered(1)),
        ],
        out_specs=pl.BlockSpec((ROW_TILE, n), lambda i: (i, 0)),
        compiler_params=_cparams("arbitrary"),
        name="proj",
    )(x2, g, w_bf16)


def _attn_a_kernel(slopes_ref, q_ref, k_ref, v_ref, o_ref, ob_ref, lse_ref):
    seq = q_ref.shape[1]
    slope = slopes_ref[pl.program_id(1)]
    scale = A_HEAD_DIM ** -0.5
    qi = lax.broadcasted_iota(jnp.int32, (BLK, BLK), 0)
    kj = lax.broadcasted_iota(jnp.int32, (BLK, BLK), 1)
    d_cur = qi - kj
    d_prev = d_cur + BLK

    for br, (window, dil) in enumerate(A_BRANCHES):
        max_delta = window // dil
        assert max_delta <= BLK and seq % (dil * BLK) == 0
        n_blk = seq // (dil * BLK)
        bias_cur = jnp.where((d_cur >= 0) & (d_cur <= max_delta),
                             -(slope * (dil * d_cur).astype(F32)), NEG_INF)
        bias_prev = jnp.where((d_prev >= 0) & (d_prev <= max_delta),
                              -(slope * (dil * d_prev).astype(F32)), NEG_INF)

        def rows(start, dil=dil):
            return pl.ds(start, BLK, stride=dil) if dil > 1 else pl.ds(start, BLK)

        def block(start, has_prev, br=br, dil=dil, bias_cur=bias_cur, bias_prev=bias_prev, rows=rows):
            q = q_ref[0, rows(start), :].astype(BF16)
            kc = k_ref[0, rows(start), :].astype(BF16)
            vc = v_ref[0, rows(start), :].astype(BF16)
            s_c = lax.dot_general(q, kc, (((1,), (1,)), ((), ())), preferred_element_type=F32) * scale + bias_cur
            m = jnp.max(s_c, axis=-1, keepdims=True)
            if has_prev:
                kp = k_ref[0, rows(start - dil * BLK), :].astype(BF16)
                vp = v_ref[0, rows(start - dil * BLK), :].astype(BF16)
                s_p = lax.dot_general(q, kp, (((1,), (1,)), ((), ())), preferred_element_type=F32) * scale + bias_prev
                m = jnp.maximum(m, jnp.max(s_p, axis=-1, keepdims=True))
            p_c = jnp.exp(s_c - m)
            den = jnp.sum(p_c, axis=-1, keepdims=True)
            acc = jnp.dot(p_c.astype(BF16), vc, preferred_element_type=F32)
            if has_prev:
                p_p = jnp.exp(s_p - m)
                den = den + jnp.sum(p_p, axis=-1, keepdims=True)
                acc = acc + jnp.dot(p_p.astype(BF16), vp, preferred_element_type=F32)
            ob_ref[br, rows(start), :] = acc / den
            lse_ref[br, rows(start), :] = jnp.broadcast_to(m + jnp.log(den), (BLK, LANES))

        def residue(r, carry, dil=dil, n_blk=n_blk, block=block):
            block(r, False)
            if n_blk > 1:
                def later(bi, c):
                    block(bi * (dil * BLK) + r, True)
                    return c
                lax.fori_loop(1, n_blk, later, 0)
            return carry

        if dil == 1:
            residue(0, 0)
        else:
            lax.fori_loop(0, dil, residue, 0)

    chunk = 256
    def mix(ci, carry):
        r = pl.ds(pl.multiple_of(ci * chunk, chunk), chunk)
        l0, l1, l2 = lse_ref[0, r, :], lse_ref[1, r, :], lse_ref[2, r, :]
        mx = jnp.maximum(jnp.maximum(l0, l1), l2)
        e0, e1, e2 = jnp.exp(l0 - mx), jnp.exp(l1 - mx), jnp.exp(l2 - mx)
        tot = e0 + e1 + e2
        out = (e0 / tot) * ob_ref[0, r, :] + (e1 / tot) * ob_ref[1, r, :] + (e2 / tot) * ob_ref[2, r, :]
        o_ref[0, r, :] = out.astype(o_ref.dtype)
        return carry
    lax.fori_loop(0, seq // chunk, mix, 0)


def _attn_a(proj3, slopes):
    b, s, _ = proj3.shape
    nh = A_HEADS
    return pl.pallas_call(
        _attn_a_kernel,
        out_shape=jax.ShapeDtypeStruct((b, s, A_WIDTH), BF16),
        grid=(b, nh),
        in_specs=[
            pl.BlockSpec(memory_space=pltpu.SMEM),
            pl.BlockSpec((1, s, A_HEAD_DIM), lambda bi, h: (bi, 0, h)),
            pl.BlockSpec((1, s, A_HEAD_DIM), lambda bi, h: (bi, 0, nh + h)),
            pl.BlockSpec((1, s, A_HEAD_DIM), lambda bi, h: (bi, 0, 2 * nh + h)),
        ],
        out_specs=pl.BlockSpec((1, s, A_HEAD_DIM), lambda bi, h: (bi, 0, h)),
        scratch_shapes=[
            pltpu.VMEM((len(A_BRANCHES), s, LANES), F32),
            pltpu.VMEM((len(A_BRANCHES), s, LANES), F32),
        ],
        compiler_params=_cparams("arbitrary", "arbitrary"),
        name="attn_a",
    )(slopes, proj3, proj3, proj3)


PAIRS_PER_KV = B_GROUP * B_HEAD_DIM // LANES


def _attn_b_kernel(slopes_ref, sinks_ref, q_ref, k_ref, v_ref, o_ref, k2_ref, v2_ref):
    seq = q_ref.shape[1]
    n_blk = seq // BLK
    pair = pl.program_id(1)
    kv_is0 = (pair // PAIRS_PER_KV) == 0
    scale = B_HEAD_DIM ** -0.5
    max_delta = B_WINDOW - 1
    lane = lax.broadcasted_iota(jnp.int32, (BLK, LANES), 1)
    lo_half = lane < B_HEAD_DIM

    def stage(bi, carry):
        r = pl.ds(pl.multiple_of(bi * BLK, BLK), BLK)
        for src, dst in ((k_ref, k2_ref), (v_ref, v2_ref)):
            t = src[0, r, :]
            t_rot = pltpu.roll(t, B_HEAD_DIM, axis=1)
            in_lo = jnp.where(kv_is0, t, t_rot)
            in_hi = jnp.where(kv_is0, t_rot, t)
            dst[bi, 0:BLK, :] = jnp.where(lo_half, in_lo, 0.0).astype(BF16)
            dst[bi, BLK:2 * BLK, :] = jnp.where(lo_half, 0.0, in_hi).astype(BF16)
        return carry
    lax.fori_loop(0, n_blk, stage, 0)

    qi = lax.broadcasted_iota(jnp.int32, (BLK, BLK), 0)
    kj = lax.broadcasted_iota(jnp.int32, (BLK, BLK), 1)
    d_cur = qi - kj
    d_prev = d_cur + BLK
    heads = (2 * pair, 2 * pair + 1)
    bias_cur, bias_prev, sink = [], [], []
    for hq in heads:
        slope = slopes_ref[hq]
        bias_cur.append(jnp.where((d_cur >= 0) & (d_cur <= max_delta), -(slope * d_cur.astype(F32)), NEG_INF))
        bias_prev.append(jnp.where((d_prev >= 0) & (d_prev <= max_delta), -(slope * d_prev.astype(F32)), NEG_INF))
        sink.append(sinks_ref[hq])

    def block(bi, has_prev):
        r = pl.ds(pl.multiple_of(bi * BLK, BLK), BLK)
        q = q_ref[0, r, :].astype(BF16)
        s_c = lax.dot_general(q, k2_ref[bi], (((1,), (1,)), ((), ())), preferred_element_type=F32)
        if has_prev:
            s_p = lax.dot_general(q, k2_ref[bi - 1], (((1,), (1,)), ((), ())), preferred_element_type=F32)
        p_c, p_p, den = [], [], []
        for j in range(2):
            sc = s_c[:, j * BLK:(j + 1) * BLK] * scale + bias_cur[j]
            m = jnp.maximum(jnp.max(sc, axis=-1, keepdims=True), sink[j])
            if has_prev:
                sp = s_p[:, j * BLK:(j + 1) * BLK] * scale + bias_prev[j]
                m = jnp.maximum(m, jnp.max(sp, axis=-1, keepdims=True))
            pc = jnp.exp(sc - m)
            dn = jnp.sum(pc, axis=-1, keepdims=True) + jnp.exp(sink[j] - m)
            p_c.append(pc.astype(BF16))
            if has_prev:
                pp = jnp.exp(sp - m)
                dn = dn + jnp.sum(pp, axis=-1, keepdims=True)
                p_p.append(pp.astype(BF16))
            den.append(dn)
        acc = jnp.dot(jnp.concatenate(p_c, axis=1), v2_ref[bi], preferred_element_type=F32)
        if has_prev:
            acc = acc + jnp.dot(jnp.concatenate(p_p, axis=1), v2_ref[bi - 1], preferred_element_type=F32)
        o_ref[0, r, :] = (acc / jnp.where(lo_half, den[0], den[1])).astype(o_ref.dtype)

    block(0, False)
    def later(bi, c):
        block(bi, True)
        return c
    lax.fori_loop(1, n_blk, later, 0)


def _attn_b(proj3, slopes, sinks):
    b, s, _ = proj3.shape
    q0 = 3 * A_WIDTH // LANES
    kblk = (3 * A_WIDTH + B_WIDTH) // LANES
    assert B_KV_HEADS * B_HEAD_DIM == LANES
    return pl.pallas_call(
        _attn_b_kernel,
        out_shape=jax.ShapeDtypeStruct((b, s, B_WIDTH), BF16),
        grid=(b, B_WIDTH // LANES),
        in_specs=[
            pl.BlockSpec(memory_space=pltpu.SMEM),
            pl.BlockSpec(memory_space=pltpu.SMEM),
            pl.BlockSpec((1, s, LANES), lambda bi, p: (bi, 0, q0 + p)),
            pl.BlockSpec((1, s, LANES), lambda bi, p: (bi, 0, kblk)),
            pl.BlockSpec((1, s, LANES), lambda bi, p: (bi, 0, kblk + 1)),
        ],
        out_specs=pl.BlockSpec((1, s, LANES), lambda bi, p: (bi, 0, p)),
        scratch_shapes=[
            pltpu.VMEM((s // BLK, 2 * BLK, LANES), BF16),
            pltpu.VMEM((s // BLK, 2 * BLK, LANES), BF16),
        ],
        compiler_params=_cparams("arbitrary", "arbitrary"),
        name="attn_b",
    )(slopes, sinks, proj3, proj3, proj3)


GRP_LANE0 = N_EXPERTS
ROUTE_FIELDS = 6


def _outproj_kernel(x_ref, ma_ref, mb_ref, wo_ref, g_ref, wr_ref, br_ref,
                    h1_ref, m_ref, route_ref, cnt_ref, carry_ref):
    tm = x_ref.shape[0]

    @pl.when(pl.program_id(0) == 0)
    def _():
        carry_ref[...] = jnp.zeros_like(carry_ref)

    h1 = (x_ref[...]
          + jnp.dot(ma_ref[...], wo_ref[0:A_WIDTH, :], preferred_element_type=F32)
          + jnp.dot(mb_ref[...], wo_ref[A_WIDTH:, :], preferred_element_type=F32))
    h1_ref[...] = h1
    m = _rms(h1, g_ref[...])
    m_ref[...] = m
    logits = jnp.dot(m.astype(BF16), wr_ref[...], preferred_element_type=F32) + br_ref[...]

    lane = lax.broadcasted_iota(jnp.int32, (tm, LANES), 1).astype(F32)
    big = float(LANES)
    gl = jnp.where((lane >= GRP_LANE0) & (lane < GRP_LANE0 + N_GROUPS), logits, NEG_INF)
    gmax = jnp.max(gl, axis=-1, keepdims=True)
    gidx = jnp.min(jnp.where(gl == gmax, lane, big), axis=-1, keepdims=True) - GRP_LANE0
    grp_w = 1.0 / jnp.sum(jnp.exp(gl - gmax), axis=-1, keepdims=True)
    e_lo = gidx * EXPERTS_PER_GROUP
    el = jnp.where((lane >= e_lo) & (lane < e_lo + EXPERTS_PER_GROUP), logits, NEG_INF)
    v1 = jnp.max(el, axis=-1, keepdims=True)
    i1 = jnp.min(jnp.where(el == v1, lane, big), axis=-1, keepdims=True)
    el2 = jnp.where(lane == i1, NEG_INF, el)
    v2 = jnp.max(el2, axis=-1, keepdims=True)
    i2 = jnp.min(jnp.where(el2 == v2, lane, big), axis=-1, keepdims=True)
    e2 = jnp.exp(v2 - v1)
    w1 = (1.0 / (1.0 + e2)) * grp_w
    w2 = (e2 / (1.0 + e2)) * grp_w

    onehot = jnp.where(lane == i1, 1.0, 0.0) + jnp.where(lane == i2, 1.0, 0.0)
    row = lax.broadcasted_iota(jnp.int32, (tm, tm), 0)
    col = lax.broadcasted_iota(jnp.int32, (tm, tm), 1)
    earlier = jnp.where(row > col, 1.0, 0.0).astype(BF16)
    before = carry_ref[0:1, :] + jnp.dot(earlier, onehot.astype(BF16), preferred_element_type=F32)
    r1 = jnp.sum(jnp.where(lane == i1, before, 0.0), axis=-1, keepdims=True)
    r2 = jnp.sum(jnp.where(lane == i2, before, 0.0), axis=-1, keepdims=True)
    carry_ref[...] = carry_ref[...] + jnp.sum(onehot, axis=0, keepdims=True)
    cnt_ref[...] = carry_ref[...]

    route = jnp.zeros((tm, LANES), F32)
    for idx, val in enumerate((i1, i2, r1, r2, w1, w2)):
        route = jnp.where(lane == float(idx), val, route)
    route_ref[...] = route


def _outproj(x2, ma, mb, wo_bf16, g, wr_bf16, br):
    t, d = x2.shape
    tm = ROW_TILE
    row_spec = lambda w: pl.BlockSpec((tm, w), lambda i: (i, 0))
    const = lambda shape: pl.BlockSpec(shape, lambda i: (0, 0))
    return pl.pallas_call(
        _outproj_kernel,
        out_shape=(
            jax.ShapeDtypeStruct((t, d), F32),
            jax.ShapeDtypeStruct((t, d), F32),
            jax.ShapeDtypeStruct((t, LANES), F32),
            jax.ShapeDtypeStruct((8, LANES), F32),
        ),
        grid=(t // tm,),
        in_specs=[
            row_spec(d), row_spec(A_WIDTH), row_spec(B_WIDTH),
            pl.BlockSpec((d, d), lambda i: (0, 0), pipeline_mode=pl.Buffered(1)),
            const((1, d)), const((d, LANES)), const((1, LANES)),
        ],
        out_specs=(row_spec(d), row_spec(d), row_spec(LANES), const((8, LANES))),
        scratch_shapes=[pltpu.VMEM((8, LANES), F32)],
        compiler_params=_cparams("arbitrary"),
        name="outproj",
    )(x2, ma, mb, wo_bf16, g, wr_bf16, br)


def _row_copy(src_ref, src_row, dst_ref, dst_row, sem):
    return pltpu.make_async_copy(src_ref.at[pl.ds(src_row, 1)], dst_ref.at[pl.ds(dst_row, 1)], sem)


def _dispatch_kernel(pos_ref, m_ref, xs_ref, sem):
    tm = m_ref.shape[0]
    base = pl.program_id(0) * (tm * TOP_K)

    def start(r, c):
        for k in range(TOP_K):
            _row_copy(m_ref, r, xs_ref, pos_ref[base + TOP_K * r + k], sem).start()
        return c
    lax.fori_loop(0, tm, start, 0)

    def wait(r, c):
        for k in range(TOP_K):
            _row_copy(m_ref, r, xs_ref, pos_ref[base + TOP_K * r + k], sem).wait()
        return c
    lax.fori_loop(0, tm, wait, 0)


def _dispatch(pos_flat, m):
    t, d = m.shape
    tm = ROW_TILE
    return pl.pallas_call(
        _dispatch_kernel,
        out_shape=jax.ShapeDtypeStruct((t * TOP_K, d), F32),
        grid_spec=pltpu.PrefetchScalarGridSpec(
            num_scalar_prefetch=1,
            grid=(t // tm,),
            in_specs=[pl.BlockSpec((tm, d), lambda i, pos: (i, 0))],
            out_specs=pl.BlockSpec(memory_space=pl.ANY),
            scratch_shapes=[pltpu.SemaphoreType.DMA],
        ),
        compiler_params=_cparams("arbitrary"),
        name="dispatch",
    )(pos_flat, m)


def _experts_kernel(tile_ref, exp_ref, lo_ref, hi_ref, first_ref, x_ref, wg_ref, wu_ref, wd_ref, y_ref):
    i = pl.program_id(0)
    lo, hi = lo_ref[i], hi_ref[i]

    @pl.when(hi > lo)
    def _():
        x = x_ref[...].astype(BF16)
        hg = jnp.dot(x, wg_ref[0].astype(BF16), preferred_element_type=F32)
        hu = jnp.dot(x, wu_ref[0].astype(BF16), preferred_element_type=F32)
        hid = (hg * jax.nn.sigmoid(hg)) * hu
        y = jnp.dot(hid.astype(BF16), wd_ref[0].astype(BF16), preferred_element_type=F32)
        row = lax.broadcasted_iota(jnp.int32, (x_ref.shape[0], 1), 0)
        mine = (row >= lo) & (row < hi)

        @pl.when(first_ref[i] == 1)
        def _():
            y_ref[...] = jnp.where(mine, y, 0.0)

        @pl.when(first_ref[i] == 0)
        def _():
            y_ref[...] = jnp.where(mine, y, y_ref[...])


def _experts(meta, xs, wg, wu, wd):
    n, d = xs.shape
    tm = EXPERT_TILE
    n_items = meta[0].shape[0]
    f = wg.shape[-1]
    return pl.pallas_call(
        _experts_kernel,
        out_shape=jax.ShapeDtypeStruct((n, d), F32),
        grid_spec=pltpu.PrefetchScalarGridSpec(
            num_scalar_prefetch=5,
            grid=(n_items,),
            in_specs=[
                pl.BlockSpec((tm, d), lambda i, tl, ex, lo, hi, fi: (tl[i], 0)),
                pl.BlockSpec((1, d, f), lambda i, tl, ex, lo, hi, fi: (ex[i], 0, 0)),
                pl.BlockSpec((1, d, f), lambda i, tl, ex, lo, hi, fi: (ex[i], 0, 0)),
                pl.BlockSpec((1, f, d), lambda i, tl, ex, lo, hi, fi: (ex[i], 0, 0)),
            ],
            out_specs=pl.BlockSpec((tm, d), lambda i, tl, ex, lo, hi, fi: (tl[i], 0)),
        ),
        compiler_params=_cparams("arbitrary"),
        name="experts",
    )(*meta, xs, wg, wu, wd)


def _expert_work_items(counts, n_rows):
    tm = EXPERT_TILE
    n_tiles = n_rows // tm
    n_items = n_tiles + N_EXPERTS - 1
    offs = jnp.cumsum(counts) - counts
    ends = offs + counts
    t_first = offs // tm
    t_last = jnp.where(counts > 0, (ends - 1) // tm, t_first - 1)
    per_e = t_last - t_first + 1
    item_end = jnp.cumsum(per_e)
    item_start = item_end - per_e
    total = item_end[-1]
    ids = jnp.arange(n_items, dtype=jnp.int32)
    e = jnp.minimum(jnp.searchsorted(item_end, ids, side="right"), N_EXPERTS - 1).astype(jnp.int32)
    tile = t_first[e] + (ids - item_start[e])
    lo = jnp.clip(offs[e] - tile * tm, 0, tm)
    hi = jnp.clip(ends[e] - tile * tm, 0, tm)
    valid = ids < total
    last = jnp.maximum(total - 1, 0)
    tile = jnp.where(valid, tile, tile[last])
    e = jnp.where(valid, e, e[last])
    lo = jnp.where(valid, lo, 0)
    hi = jnp.where(valid, hi, 0)
    prev_tile = jnp.concatenate([jnp.full((1,), -1, tile.dtype), tile[:-1]])
    first = (valid & (tile != prev_tile)).astype(jnp.int32)
    return tuple(a.astype(jnp.int32) for a in (tile, e, lo, hi, first))


def _final_kernel(pos_ref, h1_ref, route_ref, p_ref, wple_ref, wpg_ref, gple_ref, gfin_ref, ys_ref,
                  o_ref, ybuf_ref, sem):
    tm = h1_ref.shape[0]
    i = pl.program_id(0)
    n = pl.num_programs(0)
    slot = i % 2

    def gather(tile, slot_, do_start):
        base = tile * (tm * TOP_K)
        def body(r, c):
            for k in range(TOP_K):
                cp = _row_copy(ys_ref, pos_ref[base + TOP_K * r + k], ybuf_ref.at[slot_], k * tm + r, sem.at[slot_])
                if do_start:
                    cp.start()
                else:
                    cp.wait()
            return c
        lax.fori_loop(0, tm, body, 0)

    @pl.when(i == 0)
    def _():
        gather(0, 0, True)

    @pl.when(i + 1 < n)
    def _():
        gather(i + 1, 1 - slot, True)

    gather(i, slot, False)

    lane = lax.broadcasted_iota(jnp.int32, (tm, LANES), 1)
    route = route_ref[...]
    w0 = jnp.sum(jnp.where(lane == 4, route, 0.0), axis=-1, keepdims=True)
    w1 = jnp.sum(jnp.where(lane == 5, route, 0.0), axis=-1, keepdims=True)
    h2 = h1_ref[...] + (w0 * ybuf_ref[slot, 0:tm, :] + w1 * ybuf_ref[slot, tm:2 * tm, :])
    nrm = _rms(h2, gple_ref[...]).astype(BF16)
    gate = jax.nn.sigmoid(jnp.dot(nrm, wpg_ref[...], preferred_element_type=F32))
    pw = jnp.dot(p_ref[...].astype(BF16), wple_ref[...], preferred_element_type=F32)
    h3 = h2 + gate * pw
    o_ref[...] = _rms(h3, gfin_ref[...])


def _final(pos_flat, h1, route, p2, wple_bf16, wpg_bf16, gple, gfin, ys):
    t, d = h1.shape
    tm = ROW_TILE
    row_spec = lambda w: pl.BlockSpec((tm, w), lambda i, pos: (i, 0))
    const = lambda shape: pl.BlockSpec(shape, lambda i, pos: (0, 0))
    return pl.pallas_call(
        _final_kernel,
        out_shape=jax.ShapeDtypeStruct((t, d), F32),
        grid_spec=pltpu.PrefetchScalarGridSpec(
            num_scalar_prefetch=1,
            grid=(t // tm,),
            in_specs=[
                row_spec(d), row_spec(LANES), row_spec(PLE_DIM),
                const((PLE_DIM, d)),
                pl.BlockSpec((d, d), lambda i, pos: (0, 0), pipeline_mode=pl.Buffered(1)),
                const((1, d)), const((1, d)),
                pl.BlockSpec(memory_space=pl.ANY),
            ],
            out_specs=row_spec(d),
            scratch_shapes=[
                pltpu.VMEM((2, TOP_K * tm, d), F32),
                pltpu.SemaphoreType.DMA((2,)),
            ],
        ),
        compiler_params=_cparams("arbitrary"),
        name="final",
    )(pos_flat, h1, route, p2, wple_bf16, wpg_bf16, gple, gfin, ys)


def _router_weights(w_grp, b_grp, w_exp, b_exp):
    d = w_grp.shape[0]
    pad = LANES - N_EXPERTS - N_GROUPS
    wr = jnp.concatenate([w_exp.reshape(d, N_EXPERTS), w_grp, jnp.zeros((d, pad), F32)], axis=1)
    br = jnp.concatenate([b_exp.reshape(N_EXPERTS), b_grp, jnp.zeros((pad,), F32)]).reshape(1, LANES)
    return wr.astype(BF16), br.astype(F32)


def kernel(x, p, w_in, w_out, sinks, g_mix, g_moe, g_ple, g_final, w_grp, b_grp, w_exp, b_exp,
           w_gate, w_up, w_down, w_ple, w_ple_gate):
    b, s, d = x.shape
    t = b * s
    depth = w_in.shape[0]
    assert d == D_MODEL and t % ROW_TILE == 0 and (t * TOP_K) % EXPERT_TILE == 0
    slopes_a = _alibi_slopes(A_HEADS)
    slopes_b = _alibi_slopes(B_HEADS)

    h = x.reshape(t, d)
    for i in range(depth):
        proj = _proj(h, g_mix[i].reshape(1, d), w_in[i].astype(BF16))
        proj3 = proj.reshape(b, s, IN_COLS)
        mixed_a = _attn_a(proj3, slopes_a).reshape(t, A_WIDTH)
        mixed_b = _attn_b(proj3, slopes_b, sinks[i]).reshape(t, B_WIDTH)

        wr, br = _router_weights(w_grp[i], b_grp[i], w_exp[i], b_exp[i])
        h1, m, route, cnt = _outproj(h, mixed_a, mixed_b, w_out[i].astype(BF16), g_moe[i].reshape(1, d), wr, br)

        counts = cnt[0, :N_EXPERTS].astype(jnp.int32)
        offs = jnp.cumsum(counts) - counts
        eid = route[:, 0:TOP_K].astype(jnp.int32)
        rank = route[:, TOP_K:2 * TOP_K].astype(jnp.int32)
        pos_flat = (offs[eid] + rank).reshape(t * TOP_K)

        xs = _dispatch(pos_flat, m)
        ys = _experts(_expert_work_items(counts, t * TOP_K), xs, w_gate[i], w_up[i], w_down[i])
        h_next = _final(pos_flat, h1, route, p[i].reshape(t, PLE_DIM), w_ple[i].astype(BF16),
                        w_ple_gate[i].astype(BF16), g_ple[i].reshape(1, d), g_final.reshape(1, d), ys)
        if i + 1 < depth:
            raise NotImplementedError("final rmsnorm is fused into the last layer only")
        h = h_next
    return h.reshape(b, s, d)
```

```python
import functools

import numpy as np
import jax
import jax.numpy as jnp
from jax import lax
from jax.experimental import pallas as pl
from jax.experimental.pallas import tpu as pltpu

D_MODEL = 2048
PLE_DIM = 256
BLK = 128
EPS = 1e-6
A_HEAD_DIM = 128
A_WIDTH = D_MODEL // 2
A_HEADS = A_WIDTH // A_HEAD_DIM
A_BRANCHES = ((128, 1), (512, 4), (2048, 16))
B_HEAD_DIM = 64
B_WIDTH = D_MODEL - A_WIDTH
B_HEADS = B_WIDTH // B_HEAD_DIM
B_GROUP = 8
B_KV_HEADS = B_HEADS // B_GROUP
B_WINDOW = 128
IN_COLS = 3 * A_WIDTH + B_WIDTH + 2 * B_KV_HEADS * B_HEAD_DIM
N_GROUPS = 4
EXPERTS_PER_GROUP = 8
N_EXPERTS = N_GROUPS * EXPERTS_PER_GROUP
TOP_K = 2
D_EXPERT = D_MODEL // 8

LANES = 128
V7X_VMEM_LIMIT_BYTES = 56 * 1024 * 1024

ROW_TILE = 512
EXPERT_TILE = 256
GROUP = 4

F32 = jnp.float32
BF16 = jnp.bfloat16
NEG_INF = float("-inf")


def _cparams(*sem):
    return pltpu.CompilerParams(dimension_semantics=sem, vmem_limit_bytes=V7X_VMEM_LIMIT_BYTES)


def _rms(x, g):
    return x * lax.rsqrt(jnp.mean(x * x, axis=-1, keepdims=True) + EPS) * g


def _alibi_slopes(n):
    return jnp.asarray(np.array([2.0 ** (-8.0 * (i + 1) / n) for i in range(n)], dtype=np.float32))


def _proj_kernel(x_ref, g_ref, w_ref, o_ref):
    a = _rms(x_ref[...], g_ref[...]).astype(BF16)
    o_ref[...] = jnp.dot(a, w_ref[...], preferred_element_type=F32)


def _proj(x2, g, w_bf16):
    t, d = x2.shape
    n = w_bf16.shape[1]
    return pl.pallas_call(
        _proj_kernel,
        out_shape=jax.ShapeDtypeStruct((t, n), F32),
        grid=(t // ROW_TILE,),
        in_specs=[
            pl.BlockSpec((ROW_TILE, d), lambda i: (i, 0)),
            pl.BlockSpec((1, d), lambda i: (0, 0)),
            pl.BlockSpec((d, n), lambda i: (0, 0), pipeline_mode=pl.Buffered(1)),
        ],
        out_specs=pl.BlockSpec((ROW_TILE, n), lambda i: (i, 0)),
        compiler_params=_cparams("arbitrary"),
        name="proj",
    )(x2, g, w_bf16)


def _attn_a_kernel(slopes_ref, q_ref, k_ref, v_ref, o_ref, ob_ref, lse_ref):
    seq = q_ref.shape[1]
    slope = slopes_ref[pl.program_id(1)]
    scale = A_HEAD_DIM ** -0.5
    qi = lax.broadcasted_iota(jnp.int32, (BLK, BLK), 0)
    kj = lax.broadcasted_iota(jnp.int32, (BLK, BLK), 1)
    d_cur = qi - kj
    d_prev = d_cur + BLK

    for br, (window, dil) in enumerate(A_BRANCHES):
        max_delta = window // dil
        assert max_delta <= BLK and seq % (dil * BLK) == 0
        n_blk = seq // (dil * BLK)
        bias_cur = jnp.where((d_cur >= 0) & (d_cur <= max_delta),
                             -(slope * (dil * d_cur).astype(F32)), NEG_INF)
        bias_prev = jnp.where((d_prev >= 0) & (d_prev <= max_delta),
                              -(slope * (dil * d_prev).astype(F32)), NEG_INF)

        def rows(start, dil=dil):
            return pl.ds(start, BLK, stride=dil) if dil > 1 else pl.ds(start, BLK)

        def blocks(starts, has_prev, br=br, dil=dil, bias_cur=bias_cur, bias_prev=bias_prev, rows=rows):
            def nt(a, b_):
                return lax.dot_general(a, b_, (((1,), (1,)), ((), ())), preferred_element_type=F32)
            def kv(ref, start):
                cur = ref[0, rows(start), :].astype(BF16)
                if not has_prev:
                    return cur
                return jnp.concatenate([ref[0, rows(start - dil * BLK), :].astype(BF16), cur], axis=0)
            bias = jnp.concatenate([bias_prev, bias_cur], axis=1) if has_prev else bias_cur
            scores = []
            for start in starts:
                q = q_ref[0, rows(start), :].astype(BF16)
                scores.append(nt(q, kv(k_ref, start)) * scale + bias)
            probs = []
            for s in scores:
                m = jnp.max(s, axis=-1, keepdims=True)
                probs.append((m, jnp.exp(s - m).astype(BF16)))
            for start, (m, p) in zip(starts, probs):
                v = kv(v_ref, start)
                accden = jnp.dot(p, jnp.concatenate([v, jnp.ones_like(v)], axis=1), preferred_element_type=F32)
                den = accden[:, LANES:]
                ob_ref[br, rows(start), :] = accden[:, :LANES] / den
                lse_ref[br, rows(start), :] = m + jnp.log(den)

        if dil >= GROUP:
            def first(it, c, blocks=blocks):
                blocks([it * GROUP + g for g in range(GROUP)], False)
                return c
            lax.fori_loop(0, dil // GROUP, first, 0)

            def later(idx, c, blocks=blocks, dil=dil):
                bi = 1 + idx // (dil // GROUP)
                r0 = (idx % (dil // GROUP)) * GROUP
                blocks([bi * (dil * BLK) + r0 + g for g in range(GROUP)], True)
                return c
            if n_blk > 1:
                lax.fori_loop(0, (n_blk - 1) * (dil // GROUP), later, 0)
        else:
            assert dil == 1
            blocks([0], False)
            per_trip = max(g for g in range(1, GROUP + 2) if (n_blk - 1) % g == 0)

            def later(it, c, blocks=blocks, per_trip=per_trip):
                blocks([(1 + it * per_trip + g) * BLK for g in range(per_trip)], True)
                return c
            lax.fori_loop(0, (n_blk - 1) // per_trip, later, 0)

    chunk = 256
    def mix(ci, carry):
        r = pl.ds(pl.multiple_of(ci * chunk, chunk), chunk)
        l0, l1, l2 = lse_ref[0, r, :], lse_ref[1, r, :], lse_ref[2, r, :]
        mx = jnp.maximum(jnp.maximum(l0, l1), l2)
        e0, e1, e2 = jnp.exp(l0 - mx), jnp.exp(l1 - mx), jnp.exp(l2 - mx)
        tot = e0 + e1 + e2
        out = (e0 / tot) * ob_ref[0, r, :] + (e1 / tot) * ob_ref[1, r, :] + (e2 / tot) * ob_ref[2, r, :]
        o_ref[0, r, :] = out.astype(o_ref.dtype)
        return carry
    lax.fori_loop(0, seq // chunk, mix, 0)


def _attn_a(proj3, slopes):
    b, s, _ = proj3.shape
    nh = A_HEADS
    return pl.pallas_call(
        _attn_a_kernel,
        out_shape=jax.ShapeDtypeStruct((b, s, A_WIDTH), BF16),
        grid=(b, nh),
        in_specs=[
            pl.BlockSpec(memory_space=pltpu.SMEM),
            pl.BlockSpec((1, s, A_HEAD_DIM), lambda bi, h: (bi, 0, h)),
            pl.BlockSpec((1, s, A_HEAD_DIM), lambda bi, h: (bi, 0, nh + h)),
            pl.BlockSpec((1, s, A_HEAD_DIM), lambda bi, h: (bi, 0, 2 * nh + h)),
        ],
        out_specs=pl.BlockSpec((1, s, A_HEAD_DIM), lambda bi, h: (bi, 0, h)),
        scratch_shapes=[
            pltpu.VMEM((len(A_BRANCHES), s, LANES), F32),
            pltpu.VMEM((len(A_BRANCHES), s, LANES), F32),
        ],
        compiler_params=_cparams("arbitrary", "arbitrary"),
        name="attn_a",
    )(slopes, proj3, proj3, proj3)


PAIRS_PER_KV = B_GROUP * B_HEAD_DIM // LANES


def _attn_b_kernel(slopes_ref, sinks_ref, q_ref, k_ref, v_ref, o_ref, k2_ref, v2_ref):
    seq = q_ref.shape[1]
    n_blk = seq // BLK
    pair = pl.program_id(1)
    kv_is0 = (pair // PAIRS_PER_KV) == 0
    scale = B_HEAD_DIM ** -0.5
    max_delta = B_WINDOW - 1
    lane = lax.broadcasted_iota(jnp.int32, (BLK, LANES), 1)
    lo_half = lane < B_HEAD_DIM

    row2 = lax.broadcasted_iota(jnp.int32, (2 * BLK, LANES), 0)
    lane2 = lax.broadcasted_iota(jnp.int32, (2 * BLK, LANES), 1)
    head_ones = jnp.where((row2 < BLK) == (lane2 < B_HEAD_DIM), 1.0, 0.0).astype(BF16)

    @pl.when(pair % PAIRS_PER_KV == 0)
    def _():
        def stage(bi, carry):
            r = pl.ds(pl.multiple_of(bi * BLK, BLK), BLK)
            r2 = pl.ds(pl.multiple_of(bi * 2 * BLK, 2 * BLK), 2 * BLK)
            staged = []
            for src in (k_ref, v_ref):
                t = src[0, r, :]
                t_rot = pltpu.roll(t, B_HEAD_DIM, axis=1)
                in_lo = jnp.where(kv_is0, t, t_rot)
                in_hi = jnp.where(kv_is0, t_rot, t)
                staged.append(jnp.concatenate([jnp.where(lo_half, in_lo, 0.0),
                                               jnp.where(lo_half, 0.0, in_hi)], axis=0).astype(BF16))
            k2_ref[r2, :] = staged[0]
            v2_ref[r2, :] = jnp.concatenate([staged[1], head_ones], axis=1)
            return carry
        lax.fori_loop(0, n_blk, stage, 0)

    qi = lax.broadcasted_iota(jnp.int32, (BLK, BLK), 0)
    kj = lax.broadcasted_iota(jnp.int32, (BLK, BLK), 1)
    d_cur = qi - kj
    d_prev = d_cur + BLK
    assert max_delta == BLK - 1
    in_prev = kj > qi
    heads = (2 * pair, 2 * pair + 1)
    bias_fold, bias_first, sink = [], [], []
    for hq in heads:
        slope = slopes_ref[hq]
        b_cur = -(slope * d_cur.astype(F32))
        bias_fold.append(jnp.where(in_prev, -(slope * d_prev.astype(F32)), b_cur))
        bias_first.append(jnp.where(in_prev, NEG_INF, b_cur))
        sink.append(sinks_ref[hq])

    def blocks(bis, has_prev):
        def staged_rows(bi):
            n = (2 if has_prev else 1) * 2 * BLK
            first = bi - 1 if has_prev else bi
            return pl.ds(pl.multiple_of(first * 2 * BLK, 2 * BLK), n)
        scores = []
        for bi in bis:
            r = pl.ds(pl.multiple_of(bi * BLK, BLK), BLK)
            q = (q_ref[0, r, :] * scale).astype(BF16)
            scores.append(lax.dot_general(q, k2_ref[staged_rows(bi), :], (((1,), (1,)), ((), ())),
                                          preferred_element_type=F32))
        probs = []
        for s_all in scores:
            p_c, p_p, sink_term = [], [], []
            for j in range(2):
                if has_prev:
                    s_p = s_all[:, j * BLK:(j + 1) * BLK]
                    s_c = s_all[:, (2 + j) * BLK:(3 + j) * BLK]
                    s = jnp.where(in_prev, s_p, s_c) + bias_fold[j]
                else:
                    s = s_all[:, j * BLK:(j + 1) * BLK] + bias_first[j]
                m = jnp.maximum(jnp.max(s, axis=-1, keepdims=True), sink[j])
                p = jnp.exp(s - m)
                sink_term.append(jnp.exp(sink[j] - m))
                if has_prev:
                    pp = jnp.where(in_prev, p, 0.0)
                    p_p.append(pp.astype(BF16))
                    p_c.append((p - pp).astype(BF16))
                else:
                    p_c.append(p.astype(BF16))
            probs.append((jnp.concatenate(p_p + p_c, axis=1), jnp.where(lo_half, sink_term[0], sink_term[1])))
        for bi, (p, sink_den) in zip(bis, probs):
            accden = jnp.dot(p, v2_ref[staged_rows(bi), :], preferred_element_type=F32)
            r = pl.ds(pl.multiple_of(bi * BLK, BLK), BLK)
            o_ref[0, r, :] = (accden[:, :LANES] / (accden[:, LANES:] + sink_den)).astype(o_ref.dtype)

    blocks([0], False)
    per_trip = max(g for g in range(1, GROUP) if (n_blk - 1) % g == 0)

    def later(it, c):
        blocks([1 + it * per_trip + g for g in range(per_trip)], True)
        return c
    lax.fori_loop(0, (n_blk - 1) // per_trip, later, 0)


def _attn_b(proj3, slopes, sinks):
    b, s, _ = proj3.shape
    q0 = 3 * A_WIDTH // LANES
    kblk = (3 * A_WIDTH + B_WIDTH) // LANES
    assert B_KV_HEADS * B_HEAD_DIM == LANES
    return pl.pallas_call(
        _attn_b_kernel,
        out_shape=jax.ShapeDtypeStruct((b, s, B_WIDTH), BF16),
        grid=(b, B_WIDTH // LANES),
        in_specs=[
            pl.BlockSpec(memory_space=pltpu.SMEM),
            pl.BlockSpec(memory_space=pltpu.SMEM),
            pl.BlockSpec((1, s, LANES), lambda bi, p: (bi, 0, q0 + p)),
            pl.BlockSpec((1, s, LANES), lambda bi, p: (bi, 0, kblk)),
            pl.BlockSpec((1, s, LANES), lambda bi, p: (bi, 0, kblk + 1)),
        ],
        out_specs=pl.BlockSpec((1, s, LANES), lambda bi, p: (bi, 0, p)),
        scratch_shapes=[
            pltpu.VMEM((2 * s, LANES), BF16),
            pltpu.VMEM((2 * s, 2 * LANES), BF16),
        ],
        compiler_params=_cparams("arbitrary", "arbitrary"),
        name="attn_b",
    )(slopes, sinks, proj3, proj3, proj3)


GRP_LANE0 = N_EXPERTS
ROUTE_FIELDS = 6


def _outproj_kernel(x_ref, ma_ref, mb_ref, wo_ref, g_ref, wr_ref, br_ref,
                    h1_ref, m_ref, route_ref, cnt_ref, carry_ref):
    tm = x_ref.shape[0]

    @pl.when(pl.program_id(0) == 0)
    def _():
        carry_ref[...] = jnp.zeros_like(carry_ref)

    h1 = (x_ref[...]
          + jnp.dot(ma_ref[...], wo_ref[0:A_WIDTH, :], preferred_element_type=F32)
          + jnp.dot(mb_ref[...], wo_ref[A_WIDTH:, :], preferred_element_type=F32))
    h1_ref[...] = h1
    m = _rms(h1, g_ref[...])
    m_ref[...] = m
    logits = jnp.dot(m.astype(BF16), wr_ref[...], preferred_element_type=F32) + br_ref[...]

    lane = lax.broadcasted_iota(jnp.int32, (tm, LANES), 1).astype(F32)
    big = float(LANES)
    gl = jnp.where((lane >= GRP_LANE0) & (lane < GRP_LANE0 + N_GROUPS), logits, NEG_INF)
    gmax = jnp.max(gl, axis=-1, keepdims=True)
    gidx = jnp.min(jnp.where(gl == gmax, lane, big), axis=-1, keepdims=True) - GRP_LANE0
    grp_w = 1.0 / jnp.sum(jnp.exp(gl - gmax), axis=-1, keepdims=True)
    e_lo = gidx * EXPERTS_PER_GROUP
    el = jnp.where((lane >= e_lo) & (lane < e_lo + EXPERTS_PER_GROUP), logits, NEG_INF)
    v1 = jnp.max(el, axis=-1, keepdims=True)
    i1 = jnp.min(jnp.where(el == v1, lane, big), axis=-1, keepdims=True)
    el2 = jnp.where(lane == i1, NEG_INF, el)
    v2 = jnp.max(el2, axis=-1, keepdims=True)
    i2 = jnp.min(jnp.where(el2 == v2, lane, big), axis=-1, keepdims=True)
    e2 = jnp.exp(v2 - v1)
    w1 = (1.0 / (1.0 + e2)) * grp_w
    w2 = (e2 / (1.0 + e2)) * grp_w

    onehot = jnp.where(lane == i1, 1.0, 0.0) + jnp.where(lane == i2, 1.0, 0.0)
    row = lax.broadcasted_iota(jnp.int32, (tm, tm), 0)
    col = lax.broadcasted_iota(jnp.int32, (tm, tm), 1)
    earlier = jnp.where(row > col, 1.0, 0.0).astype(BF16)
    before = carry_ref[0:1, :] + jnp.dot(earlier, onehot.astype(BF16), preferred_element_type=F32)
    r1 = jnp.sum(jnp.where(lane == i1, before, 0.0), axis=-1, keepdims=True)
    r2 = jnp.sum(jnp.where(lane == i2, before, 0.0), axis=-1, keepdims=True)
    carry_ref[...] = carry_ref[...] + jnp.sum(onehot, axis=0, keepdims=True)
    cnt_ref[...] = carry_ref[...]

    route = jnp.zeros((tm, LANES), F32)
    for idx, val in enumerate((i1, i2, r1, r2, w1, w2)):
        route = jnp.where(lane == float(idx), val, route)
    route_ref[...] = route


def _outproj(x2, ma, mb, wo_bf16, g, wr_bf16, br):
    t, d = x2.shape
    tm = ROW_TILE
    row_spec = lambda w: pl.BlockSpec((tm, w), lambda i: (i, 0))
    const = lambda shape: pl.BlockSpec(shape, lambda i: (0, 0))
    return pl.pallas_call(
        _outproj_kernel,
        out_shape=(
            jax.ShapeDtypeStruct((t, d), F32),
            jax.ShapeDtypeStruct((t, d), F32),
            jax.ShapeDtypeStruct((t, LANES), F32),
            jax.ShapeDtypeStruct((8, LANES), F32),
        ),
        grid=(t // tm,),
        in_specs=[
            row_spec(d), row_spec(A_WIDTH), row_spec(B_WIDTH),
            pl.BlockSpec((d, d), lambda i: (0, 0), pipeline_mode=pl.Buffered(1)),
            const((1, d)), const((d, LANES)), const((1, LANES)),
        ],
        out_specs=(row_spec(d), row_spec(d), row_spec(LANES), const((8, LANES))),
        scratch_shapes=[pltpu.VMEM((8, LANES), F32)],
        compiler_params=_cparams("arbitrary"),
        name="outproj",
    )(x2, ma, mb, wo_bf16, g, wr_bf16, br)


def _row_copy(src_ref, src_row, dst_ref, dst_row, sem):
    return pltpu.make_async_copy(src_ref.at[pl.ds(src_row, 1)], dst_ref.at[pl.ds(dst_row, 1)], sem)


def _dispatch_kernel(pos_ref, m_ref, xs_ref, sem):
    tm = m_ref.shape[0]
    base = pl.program_id(0) * (tm * TOP_K)

    def start(r, c):
        for k in range(TOP_K):
            _row_copy(m_ref, r, xs_ref, pos_ref[base + TOP_K * r + k], sem).start()
        return c
    lax.fori_loop(0, tm, start, 0)

    def wait(r, c):
        for k in range(TOP_K):
            _row_copy(m_ref, r, xs_ref, pos_ref[base + TOP_K * r + k], sem).wait()
        return c
    lax.fori_loop(0, tm, wait, 0)


def _dispatch(pos_flat, m):
    t, d = m.shape
    tm = ROW_TILE
    return pl.pallas_call(
        _dispatch_kernel,
        out_shape=jax.ShapeDtypeStruct((t * TOP_K, d), F32),
        grid_spec=pltpu.PrefetchScalarGridSpec(
            num_scalar_prefetch=1,
            grid=(t // tm,),
            in_specs=[pl.BlockSpec((tm, d), lambda i, pos: (i, 0))],
            out_specs=pl.BlockSpec(memory_space=pl.ANY),
            scratch_shapes=[pltpu.SemaphoreType.DMA],
        ),
        compiler_params=_cparams("arbitrary"),
        name="dispatch",
    )(pos_flat, m)


def _experts_kernel(tile_ref, exp_ref, lo_ref, hi_ref, first_ref, x_ref, wg_ref, wu_ref, wd_ref, y_ref):
    i = pl.program_id(0)
    lo, hi = lo_ref[i], hi_ref[i]

    @pl.when(hi > lo)
    def _():
        x = x_ref[...].astype(BF16)
        hg = jnp.dot(x, wg_ref[0].astype(BF16), preferred_element_type=F32)
        hu = jnp.dot(x, wu_ref[0].astype(BF16), preferred_element_type=F32)
        hid = (hg * jax.nn.sigmoid(hg)) * hu
        y = jnp.dot(hid.astype(BF16), wd_ref[0].astype(BF16), preferred_element_type=F32)
        row = lax.broadcasted_iota(jnp.int32, (x_ref.shape[0], 1), 0)
        mine = (row >= lo) & (row < hi)

        @pl.when(first_ref[i] == 1)
        def _():
            y_ref[...] = jnp.where(mine, y, 0.0)

        @pl.when(first_ref[i] == 0)
        def _():
            y_ref[...] = jnp.where(mine, y, y_ref[...])


def _experts(meta, xs, wg, wu, wd):
    n, d = xs.shape
    tm = EXPERT_TILE
    n_items = meta[0].shape[0]
    f = wg.shape[-1]
    return pl.pallas_call(
        _experts_kernel,
        out_shape=jax.ShapeDtypeStruct((n, d), F32),
        grid_spec=pltpu.PrefetchScalarGridSpec(
            num_scalar_prefetch=5,
            grid=(n_items,),
            in_specs=[
                pl.BlockSpec((tm, d), lambda i, tl, ex, lo, hi, fi: (tl[i], 0)),
                pl.BlockSpec((1, d, f), lambda i, tl, ex, lo, hi, fi: (ex[i], 0, 0)),
                pl.BlockSpec((1, d, f), lambda i, tl, ex, lo, hi, fi: (ex[i], 0, 0)),
                pl.BlockSpec((1, f, d), lambda i, tl, ex, lo, hi, fi: (ex[i], 0, 0)),
            ],
            out_specs=pl.BlockSpec((tm, d), lambda i, tl, ex, lo, hi, fi: (tl[i], 0)),
        ),
        compiler_params=_cparams("arbitrary"),
        name="experts",
    )(*meta, xs, wg, wu, wd)


def _expert_work_items(counts, n_rows):
    tm = EXPERT_TILE
    n_tiles = n_rows // tm
    n_items = n_tiles + N_EXPERTS - 1
    offs = jnp.cumsum(counts) - counts
    ends = offs + counts
    t_first = offs // tm
    t_last = jnp.where(counts > 0, (ends - 1) // tm, t_first - 1)
    per_e = t_last - t_first + 1
    item_end = jnp.cumsum(per_e)
    item_start = item_end - per_e
    total = item_end[-1]
    ids = jnp.arange(n_items, dtype=jnp.int32)
    e = jnp.minimum(jnp.sum(item_end[None, :] <= ids[:, None], axis=1), N_EXPERTS - 1).astype(jnp.int32)
    tile = t_first[e] + (ids - item_start[e])
    lo = jnp.clip(offs[e] - tile * tm, 0, tm)
    hi = jnp.clip(ends[e] - tile * tm, 0, tm)
    valid = ids < total
    last = jnp.maximum(total - 1, 0)
    tile = jnp.where(valid, tile, tile[last])
    e = jnp.where(valid, e, e[last])
    lo = jnp.where(valid, lo, 0)
    hi = jnp.where(valid, hi, 0)
    prev_tile = jnp.concatenate([jnp.full((1,), -1, tile.dtype), tile[:-1]])
    first = (valid & (tile != prev_tile)).astype(jnp.int32)
    return tuple(a.astype(jnp.int32) for a in (tile, e, lo, hi, first))


def _final_kernel(pos_ref, h1_ref, route_ref, p_ref, wple_ref, wpg_ref, gple_ref, gfin_ref, ys_ref,
                  o_ref, ybuf_ref, sem):
    tm = h1_ref.shape[0]
    i = pl.program_id(0)
    n = pl.num_programs(0)
    slot = i % 2

    def gather(tile, slot_, do_start):
        base = tile * (tm * TOP_K)
        def body(r, c):
            for k in range(TOP_K):
                cp = _row_copy(ys_ref, pos_ref[base + TOP_K * r + k], ybuf_ref.at[slot_], k * tm + r, sem.at[slot_])
                if do_start:
                    cp.start()
                else:
                    cp.wait()
            return c
        lax.fori_loop(0, tm, body, 0)

    @pl.when(i == 0)
    def _():
        gather(0, 0, True)

    @pl.when(i + 1 < n)
    def _():
        gather(i + 1, 1 - slot, True)

    gather(i, slot, False)

    lane = lax.broadcasted_iota(jnp.int32, (tm, LANES), 1)
    route = route_ref[...]
    w0 = jnp.sum(jnp.where(lane == 4, route, 0.0), axis=-1, keepdims=True)
    w1 = jnp.sum(jnp.where(lane == 5, route, 0.0), axis=-1, keepdims=True)
    h2 = h1_ref[...] + (w0 * ybuf_ref[slot, 0:tm, :] + w1 * ybuf_ref[slot, tm:2 * tm, :])
    nrm = _rms(h2, gple_ref[...]).astype(BF16)
    gate = jax.nn.sigmoid(jnp.dot(nrm, wpg_ref[...], preferred_element_type=F32))
    pw = jnp.dot(p_ref[...].astype(BF16), wple_ref[...], preferred_element_type=F32)
    h3 = h2 + gate * pw
    o_ref[...] = _rms(h3, gfin_ref[...])


def _final(pos_flat, h1, route, p2, wple_bf16, wpg_bf16, gple, gfin, ys):
    t, d = h1.shape
    tm = ROW_TILE
    row_spec = lambda w: pl.BlockSpec((tm, w), lambda i, pos: (i, 0))
    const = lambda shape: pl.BlockSpec(shape, lambda i, pos: (0, 0))
    return pl.pallas_call(
        _final_kernel,
        out_shape=jax.ShapeDtypeStruct((t, d), F32),
        grid_spec=pltpu.PrefetchScalarGridSpec(
            num_scalar_prefetch=1,
            grid=(t // tm,),
            in_specs=[
                row_spec(d), row_spec(LANES), row_spec(PLE_DIM),
                const((PLE_DIM, d)),
                pl.BlockSpec((d, d), lambda i, pos: (0, 0), pipeline_mode=pl.Buffered(1)),
                const((1, d)), const((1, d)),
                pl.BlockSpec(memory_space=pl.ANY),
            ],
            out_specs=row_spec(d),
            scratch_shapes=[
                pltpu.VMEM((2, TOP_K * tm, d), F32),
                pltpu.SemaphoreType.DMA((2,)),
            ],
        ),
        compiler_params=_cparams("arbitrary"),
        name="final",
    )(pos_flat, h1, route, p2, wple_bf16, wpg_bf16, gple, gfin, ys)


def _router_weights(w_grp, b_grp, w_exp, b_exp):
    d = w_grp.shape[0]
    pad = LANES - N_EXPERTS - N_GROUPS
    wr = jnp.concatenate([w_exp.reshape(d, N_EXPERTS), w_grp, jnp.zeros((d, pad), F32)], axis=1)
    br = jnp.concatenate([b_exp.reshape(N_EXPERTS), b_grp, jnp.zeros((pad,), F32)]).reshape(1, LANES)
    return wr.astype(BF16), br.astype(F32)


def kernel(x, p, w_in, w_out, sinks, g_mix, g_moe, g_ple, g_final, w_grp, b_grp, w_exp, b_exp,
           w_gate, w_up, w_down, w_ple, w_ple_gate):
    b, s, d = x.shape
    t = b * s
    depth = w_in.shape[0]
    assert d == D_MODEL and t % ROW_TILE == 0 and (t * TOP_K) % EXPERT_TILE == 0
    assert depth == 1, "the final rmsnorm is fused into the (single) layer's last kernel"
    slopes_a = _alibi_slopes(A_HEADS)
    slopes_b = _alibi_slopes(B_HEADS)

    h = x.reshape(t, d)
    for i in range(depth):
        proj = _proj(h, g_mix[i].reshape(1, d), w_in[i].astype(BF16))
        proj3 = proj.reshape(b, s, IN_COLS)
        mixed_a = _attn_a(proj3, slopes_a).reshape(t, A_WIDTH)
        mixed_b = _attn_b(proj3, slopes_b, sinks[i]).reshape(t, B_WIDTH)

        wr, br = _router_weights(w_grp[i], b_grp[i], w_exp[i], b_exp[i])
        h1, m, route, cnt = _outproj(h, mixed_a, mixed_b, w_out[i].astype(BF16), g_moe[i].reshape(1, d), wr, br)

        counts = cnt[0, :N_EXPERTS].astype(jnp.int32)
        offs = jnp.cumsum(counts) - counts
        eid = route[:, 0:TOP_K].astype(jnp.int32)
        rank = route[:, TOP_K:2 * TOP_K].astype(jnp.int32)
        pos_flat = (offs[eid] + rank).reshape(t * TOP_K)

        xs = _dispatch(pos_flat, m)
        ys = _experts(_expert_work_items(counts, t * TOP_K), xs, w_gate[i], w_up[i], w_down[i])
        h_next = _final(pos_flat, h1, route, p[i].reshape(t, PLE_DIM), w_ple[i].astype(BF16),
                        w_ple_gate[i].astype(BF16), g_ple[i].reshape(1, d), g_final.reshape(1, d), ys)
        h = h_next
    return h.reshape(b, s, d)
```

```python
import functools

import numpy as np
import jax
import jax.numpy as jnp
from jax import lax
from jax.experimental import pallas as pl
from jax.experimental.pallas import tpu as pltpu

D_MODEL = 2048
PLE_DIM = 256
BLK = 128
EPS = 1e-6
A_HEAD_DIM = 128
A_WIDTH = D_MODEL // 2
A_HEADS = A_WIDTH // A_HEAD_DIM
A_BRANCHES = ((128, 1), (512, 4), (2048, 16))
B_HEAD_DIM = 64
B_WIDTH = D_MODEL - A_WIDTH
B_HEADS = B_WIDTH // B_HEAD_DIM
B_GROUP = 8
B_KV_HEADS = B_HEADS // B_GROUP
B_WINDOW = 128
IN_COLS = 3 * A_WIDTH + B_WIDTH + 2 * B_KV_HEADS * B_HEAD_DIM
N_GROUPS = 4
EXPERTS_PER_GROUP = 8
N_EXPERTS = N_GROUPS * EXPERTS_PER_GROUP
TOP_K = 2
D_EXPERT = D_MODEL // 8

LANES = 128
SUBLANES = 8
V7X_VMEM_LIMIT_BYTES = 56 * 1024 * 1024

ROW_TILE = 512
EXPERT_TILE = 256
GROUP = 4
DMA_UNROLL = 16

F32 = jnp.float32
BF16 = jnp.bfloat16
NEG_INF = float("-inf")


def _cparams(*sem):
    return pltpu.CompilerParams(dimension_semantics=sem, vmem_limit_bytes=V7X_VMEM_LIMIT_BYTES)


def _rms(x, g):
    return x * lax.rsqrt(jnp.mean(x * x, axis=-1, keepdims=True) + EPS) * g


def _pack_halves(x):
    w = x.shape[1] // 2
    as_bits = lambda v: lax.bitcast_convert_type(v.astype(BF16).astype(F32), jnp.uint32)
    return (as_bits(x[:, w:]) & jnp.uint32(0xFFFF0000)) | (as_bits(x[:, :w]) >> 16)


def _unpack_halves(u):
    lo = lax.bitcast_convert_type(u << 16, F32)
    hi = lax.bitcast_convert_type(u & jnp.uint32(0xFFFF0000), F32)
    return lo, hi


def _alibi_slopes(n):
    return jnp.asarray(np.array([2.0 ** (-8.0 * (i + 1) / n) for i in range(n)], dtype=np.float32))


def _proj_kernel(x_ref, g_ref, w_ref, o_ref):
    a = _rms(x_ref[...], g_ref[...]).astype(BF16)
    o_ref[...] = jnp.dot(a, w_ref[...], preferred_element_type=F32)


def _proj(x2, g, w_bf16):
    t, d = x2.shape
    n = w_bf16.shape[1]
    return pl.pallas_call(
        _proj_kernel,
        out_shape=jax.ShapeDtypeStruct((t, n), F32),
        grid=(t // ROW_TILE,),
        in_specs=[
            pl.BlockSpec((ROW_TILE, d), lambda i: (i, 0)),
            pl.BlockSpec((1, d), lambda i: (0, 0)),
            pl.BlockSpec((d, n), lambda i: (0, 0), pipeline_mode=pl.Buffered(1)),
        ],
        out_specs=pl.BlockSpec((ROW_TILE, n), lambda i: (i, 0)),
        compiler_params=_cparams("arbitrary"),
        name="proj",
    )(x2, g, w_bf16)


def _attn_a_kernel(slopes_ref, q_ref, k_ref, v_ref, o_ref, ob_ref, lse_ref):
    seq = q_ref.shape[1]
    slope = slopes_ref[pl.program_id(1)]
    scale = A_HEAD_DIM ** -0.5
    qi = lax.broadcasted_iota(jnp.int32, (BLK, BLK), 0)
    kj = lax.broadcasted_iota(jnp.int32, (BLK, BLK), 1)
    d_cur = qi - kj
    d_prev = d_cur + BLK

    for br, (window, dil) in enumerate(A_BRANCHES):
        max_delta = window // dil
        assert max_delta <= BLK and seq % (dil * BLK) == 0
        n_blk = seq // (dil * BLK)
        bias_cur = jnp.where((d_cur >= 0) & (d_cur <= max_delta),
                             -(slope * (dil * d_cur).astype(F32)), NEG_INF)
        bias_prev = jnp.where((d_prev >= 0) & (d_prev <= max_delta),
                              -(slope * (dil * d_prev).astype(F32)), NEG_INF)

        def rows(start, dil=dil):
            return pl.ds(start, BLK, stride=dil) if dil > 1 else pl.ds(start, BLK)

        def blocks(starts, has_prev, br=br, dil=dil, bias_cur=bias_cur, bias_prev=bias_prev, rows=rows):
            def nt(a, b_):
                return lax.dot_general(a, b_, (((1,), (1,)), ((), ())), preferred_element_type=F32)
            def kv(ref, start):
                cur = ref[0, rows(start), :].astype(BF16)
                if not has_prev:
                    return cur
                return jnp.concatenate([ref[0, rows(start - dil * BLK), :].astype(BF16), cur], axis=0)
            bias = jnp.concatenate([bias_prev, bias_cur], axis=1) if has_prev else bias_cur
            scores = []
            for start in starts:
                q = q_ref[0, rows(start), :].astype(BF16)
                scores.append(nt(q, kv(k_ref, start)) * scale + bias)
            probs = []
            for s in scores:
                m = jnp.max(s, axis=-1, keepdims=True)
                probs.append((m, jnp.exp(s - m).astype(BF16)))
            for start, (m, p) in zip(starts, probs):
                v = kv(v_ref, start)
                accden = jnp.dot(p, jnp.concatenate([v, jnp.ones_like(v)], axis=1), preferred_element_type=F32)
                den = accden[:, LANES:]
                ob_ref[br, rows(start), :] = accden[:, :LANES] / den
                lse_ref[br, rows(start), :] = m + jnp.log(den)

        if dil >= GROUP:
            def first(it, c, blocks=blocks):
                blocks([it * GROUP + g for g in range(GROUP)], False)
                return c
            lax.fori_loop(0, dil // GROUP, first, 0)

            def later(idx, c, blocks=blocks, dil=dil):
                bi = 1 + idx // (dil // GROUP)
                r0 = (idx % (dil // GROUP)) * GROUP
                blocks([bi * (dil * BLK) + r0 + g for g in range(GROUP)], True)
                return c
            if n_blk > 1:
                lax.fori_loop(0, (n_blk - 1) * (dil // GROUP), later, 0)
        else:
            assert dil == 1
            blocks([0], False)
            per_trip = max(g for g in range(1, GROUP + 2) if (n_blk - 1) % g == 0)

            def later(it, c, blocks=blocks, per_trip=per_trip):
                blocks([(1 + it * per_trip + g) * BLK for g in range(per_trip)], True)
                return c
            lax.fori_loop(0, (n_blk - 1) // per_trip, later, 0)

    chunk = 256
    def mix(ci, carry):
        r = pl.ds(pl.multiple_of(ci * chunk, chunk), chunk)
        l0, l1, l2 = lse_ref[0, r, :], lse_ref[1, r, :], lse_ref[2, r, :]
        mx = jnp.maximum(jnp.maximum(l0, l1), l2)
        e0, e1, e2 = jnp.exp(l0 - mx), jnp.exp(l1 - mx), jnp.exp(l2 - mx)
        tot = e0 + e1 + e2
        out = (e0 / tot) * ob_ref[0, r, :] + (e1 / tot) * ob_ref[1, r, :] + (e2 / tot) * ob_ref[2, r, :]
        o_ref[0, r, :] = out.astype(o_ref.dtype)
        return carry
    lax.fori_loop(0, seq // chunk, mix, 0)


def _attn_a(proj3, slopes):
    b, s, _ = proj3.shape
    nh = A_HEADS
    return pl.pallas_call(
        _attn_a_kernel,
        out_shape=jax.ShapeDtypeStruct((b, s, A_WIDTH), BF16),
        grid=(b, nh),
        in_specs=[
            pl.BlockSpec(memory_space=pltpu.SMEM),
            pl.BlockSpec((1, s, A_HEAD_DIM), lambda bi, h: (bi, 0, h)),
            pl.BlockSpec((1, s, A_HEAD_DIM), lambda bi, h: (bi, 0, nh + h)),
            pl.BlockSpec((1, s, A_HEAD_DIM), lambda bi, h: (bi, 0, 2 * nh + h)),
        ],
        out_specs=pl.BlockSpec((1, s, A_HEAD_DIM), lambda bi, h: (bi, 0, h)),
        scratch_shapes=[
            pltpu.VMEM((len(A_BRANCHES), s, LANES), F32),
            pltpu.VMEM((len(A_BRANCHES), s, LANES), F32),
        ],
        compiler_params=_cparams("arbitrary", "arbitrary"),
        name="attn_a",
    )(slopes, proj3, proj3, proj3)


PAIRS_PER_KV = B_GROUP * B_HEAD_DIM // LANES


def _attn_b_kernel(slopes_ref, sinks_ref, q_ref, k_ref, v_ref, o_ref, k2_ref, v2_ref):
    seq = q_ref.shape[1]
    n_blk = seq // BLK
    pair = pl.program_id(1)
    kv_is0 = (pair // PAIRS_PER_KV) == 0
    scale = B_HEAD_DIM ** -0.5
    max_delta = B_WINDOW - 1
    lane = lax.broadcasted_iota(jnp.int32, (BLK, LANES), 1)
    lo_half = lane < B_HEAD_DIM

    row2 = lax.broadcasted_iota(jnp.int32, (2 * BLK, LANES), 0)
    lane2 = lax.broadcasted_iota(jnp.int32, (2 * BLK, LANES), 1)
    head_ones = jnp.where((row2 < BLK) == (lane2 < B_HEAD_DIM), 1.0, 0.0).astype(BF16)

    @pl.when(pair % PAIRS_PER_KV == 0)
    def _():
        def stage(bi, carry):
            r = pl.ds(pl.multiple_of(bi * BLK, BLK), BLK)
            r2 = pl.ds(pl.multiple_of(bi * 2 * BLK, 2 * BLK), 2 * BLK)
            staged = []
            for src in (k_ref, v_ref):
                t = src[0, r, :]
                t_rot = pltpu.roll(t, B_HEAD_DIM, axis=1)
                in_lo = jnp.where(kv_is0, t, t_rot)
                in_hi = jnp.where(kv_is0, t_rot, t)
                staged.append(jnp.concatenate([jnp.where(lo_half, in_lo, 0.0),
                                               jnp.where(lo_half, 0.0, in_hi)], axis=0).astype(BF16))
            k2_ref[r2, :] = staged[0]
            v2_ref[r2, :] = jnp.concatenate([staged[1], head_ones], axis=1)
            return carry
        lax.fori_loop(0, n_blk, stage, 0)

    qi = lax.broadcasted_iota(jnp.int32, (BLK, BLK), 0)
    kj = lax.broadcasted_iota(jnp.int32, (BLK, BLK), 1)
    d_cur = qi - kj
    d_prev = d_cur + BLK
    assert max_delta == BLK - 1
    in_prev = kj > qi
    heads = (2 * pair, 2 * pair + 1)
    bias_fold, bias_first, sink = [], [], []
    for hq in heads:
        slope = slopes_ref[hq]
        b_cur = -(slope * d_cur.astype(F32))
        bias_fold.append(jnp.where(in_prev, -(slope * d_prev.astype(F32)), b_cur))
        bias_first.append(jnp.where(in_prev, NEG_INF, b_cur))
        sink.append(sinks_ref[hq])

    def blocks(bis, has_prev):
        def staged_rows(bi):
            n = (2 if has_prev else 1) * 2 * BLK
            first = bi - 1 if has_prev else bi
            return pl.ds(pl.multiple_of(first * 2 * BLK, 2 * BLK), n)
        scores = []
        for bi in bis:
            r = pl.ds(pl.multiple_of(bi * BLK, BLK), BLK)
            q = (q_ref[0, r, :] * scale).astype(BF16)
            scores.append(lax.dot_general(q, k2_ref[staged_rows(bi), :], (((1,), (1,)), ((), ())),
                                          preferred_element_type=F32))
        probs = []
        for s_all in scores:
            p_c, p_p, sink_term = [], [], []
            for j in range(2):
                if has_prev:
                    s_p = s_all[:, j * BLK:(j + 1) * BLK]
                    s_c = s_all[:, (2 + j) * BLK:(3 + j) * BLK]
                    s = jnp.where(in_prev, s_p, s_c) + bias_fold[j]
                else:
                    s = s_all[:, j * BLK:(j + 1) * BLK] + bias_first[j]
                m = jnp.maximum(jnp.max(s, axis=-1, keepdims=True), sink[j])
                p = jnp.exp(s - m)
                sink_term.append(jnp.exp(sink[j] - m))
                if has_prev:
                    pp = jnp.where(in_prev, p, 0.0)
                    p_p.append(pp.astype(BF16))
                    p_c.append((p - pp).astype(BF16))
                else:
                    p_c.append(p.astype(BF16))
            probs.append((jnp.concatenate(p_p + p_c, axis=1), jnp.where(lo_half, sink_term[0], sink_term[1])))
        for bi, (p, sink_den) in zip(bis, probs):
            accden = jnp.dot(p, v2_ref[staged_rows(bi), :], preferred_element_type=F32)
            r = pl.ds(pl.multiple_of(bi * BLK, BLK), BLK)
            o_ref[0, r, :] = (accden[:, :LANES] / (accden[:, LANES:] + sink_den)).astype(o_ref.dtype)

    blocks([0], False)
    per_trip = max(g for g in range(1, GROUP) if (n_blk - 1) % g == 0)

    def later(it, c):
        blocks([1 + it * per_trip + g for g in range(per_trip)], True)
        return c
    lax.fori_loop(0, (n_blk - 1) // per_trip, later, 0)


def _attn_b(proj3, slopes, sinks):
    b, s, _ = proj3.shape
    q0 = 3 * A_WIDTH // LANES
    kblk = (3 * A_WIDTH + B_WIDTH) // LANES
    assert B_KV_HEADS * B_HEAD_DIM == LANES
    return pl.pallas_call(
        _attn_b_kernel,
        out_shape=jax.ShapeDtypeStruct((b, s, B_WIDTH), BF16),
        grid=(b, B_WIDTH // LANES),
        in_specs=[
            pl.BlockSpec(memory_space=pltpu.SMEM),
            pl.BlockSpec(memory_space=pltpu.SMEM),
            pl.BlockSpec((1, s, LANES), lambda bi, p: (bi, 0, q0 + p)),
            pl.BlockSpec((1, s, LANES), lambda bi, p: (bi, 0, kblk)),
            pl.BlockSpec((1, s, LANES), lambda bi, p: (bi, 0, kblk + 1)),
        ],
        out_specs=pl.BlockSpec((1, s, LANES), lambda bi, p: (bi, 0, p)),
        scratch_shapes=[
            pltpu.VMEM((2 * s, LANES), BF16),
            pltpu.VMEM((2 * s, 2 * LANES), BF16),
        ],
        compiler_params=_cparams("arbitrary", "arbitrary"),
        name="attn_b",
    )(slopes, sinks, proj3, proj3, proj3)


GRP_LANE0 = N_EXPERTS
ROUTE_FIELDS = 6


def _outproj_kernel(x_ref, ma_ref, mb_ref, wo_ref, g_ref, wr_ref, br_ref,
                    h1_ref, m_ref, route_ref, cnt_ref, carry_ref):
    tm = x_ref.shape[0]

    @pl.when(pl.program_id(0) == 0)
    def _():
        carry_ref[...] = jnp.zeros_like(carry_ref)

    h1 = (x_ref[...]
          + jnp.dot(ma_ref[...], wo_ref[0:A_WIDTH, :], preferred_element_type=F32)
          + jnp.dot(mb_ref[...], wo_ref[A_WIDTH:, :], preferred_element_type=F32))
    h1_ref[...] = h1
    m = _rms(h1, g_ref[...])
    m_ref[...] = _pack_halves(m).reshape(m_ref.shape)
    logits = jnp.dot(m.astype(BF16), wr_ref[...], preferred_element_type=F32) + br_ref[...]

    lane = lax.broadcasted_iota(jnp.int32, (tm, LANES), 1).astype(F32)
    big = float(LANES)
    gl = jnp.where((lane >= GRP_LANE0) & (lane < GRP_LANE0 + N_GROUPS), logits, NEG_INF)
    gmax = jnp.max(gl, axis=-1, keepdims=True)
    gidx = jnp.min(jnp.where(gl == gmax, lane, big), axis=-1, keepdims=True) - GRP_LANE0
    grp_w = 1.0 / jnp.sum(jnp.exp(gl - gmax), axis=-1, keepdims=True)
    e_lo = gidx * EXPERTS_PER_GROUP
    el = jnp.where((lane >= e_lo) & (lane < e_lo + EXPERTS_PER_GROUP), logits, NEG_INF)
    v1 = jnp.max(el, axis=-1, keepdims=True)
    i1 = jnp.min(jnp.where(el == v1, lane, big), axis=-1, keepdims=True)
    el2 = jnp.where(lane == i1, NEG_INF, el)
    v2 = jnp.max(el2, axis=-1, keepdims=True)
    i2 = jnp.min(jnp.where(el2 == v2, lane, big), axis=-1, keepdims=True)
    e2 = jnp.exp(v2 - v1)
    w1 = (1.0 / (1.0 + e2)) * grp_w
    w2 = (e2 / (1.0 + e2)) * grp_w

    onehot = jnp.where(lane == i1, 1.0, 0.0) + jnp.where(lane == i2, 1.0, 0.0)
    row = lax.broadcasted_iota(jnp.int32, (tm, tm), 0)
    col = lax.broadcasted_iota(jnp.int32, (tm, tm), 1)
    earlier = jnp.where(row > col, 1.0, 0.0).astype(BF16)
    before = carry_ref[0:1, :] + jnp.dot(earlier, onehot.astype(BF16), preferred_element_type=F32)
    r1 = jnp.sum(jnp.where(lane == i1, before, 0.0), axis=-1, keepdims=True)
    r2 = jnp.sum(jnp.where(lane == i2, before, 0.0), axis=-1, keepdims=True)
    carry_ref[...] = carry_ref[...] + jnp.sum(onehot, axis=0, keepdims=True)
    cnt_ref[...] = carry_ref[...]

    route = jnp.zeros((tm, LANES), F32)
    for idx, val in enumerate((i1, i2, r1, r2, w1, w2)):
        route = jnp.where(lane == float(idx), val, route)
    route_ref[...] = route


def _outproj(x2, ma, mb, wo_bf16, g, wr_bf16, br):
    t, d = x2.shape
    tm = ROW_TILE
    row_spec = lambda w: pl.BlockSpec((tm, w), lambda i: (i, 0))
    const = lambda shape: pl.BlockSpec(shape, lambda i: (0, 0))
    return pl.pallas_call(
        _outproj_kernel,
        out_shape=(
            jax.ShapeDtypeStruct((t, d), F32),
            jax.ShapeDtypeStruct((t // SUBLANES, SUBLANES, d // 2), jnp.uint32),
            jax.ShapeDtypeStruct((t, LANES), F32),
            jax.ShapeDtypeStruct((8, LANES), F32),
        ),
        grid=(t // tm,),
        in_specs=[
            row_spec(d), row_spec(A_WIDTH), row_spec(B_WIDTH),
            pl.BlockSpec((d, d), lambda i: (0, 0), pipeline_mode=pl.Buffered(1)),
            const((1, d)), const((d, LANES)), const((1, LANES)),
        ],
        out_specs=(row_spec(d),
                   pl.BlockSpec((tm // SUBLANES, SUBLANES, d // 2), lambda i: (i, 0, 0)),
                   row_spec(LANES), const((8, LANES))),
        scratch_shapes=[pltpu.VMEM((8, LANES), F32)],
        compiler_params=_cparams("arbitrary"),
        name="outproj",
    )(x2, ma, mb, wo_bf16, g, wr_bf16, br)


def _dispatch_kernel(pos_ref, m_ref, xs_ref, sem):
    groups = m_ref.shape[0]
    base = pl.program_id(0) * (groups * SUBLANES * TOP_K)

    def start(g, c):
        for j in range(SUBLANES):
            for k in range(TOP_K):
                dst = pos_ref[base + TOP_K * (g * SUBLANES + j) + k]
                pltpu.make_async_copy(m_ref.at[g, pl.ds(j, 1)], xs_ref.at[pl.ds(dst, 1)], sem).start()
        return c
    lax.fori_loop(0, groups, start, 0, unroll=DMA_UNROLL // SUBLANES)

    def wait(g, c):
        for _ in range(SUBLANES * TOP_K):
            pltpu.make_async_copy(m_ref.at[0, pl.ds(0, 1)], xs_ref.at[pl.ds(0, 1)], sem).wait()
        return c
    lax.fori_loop(0, groups, wait, 0, unroll=DMA_UNROLL // SUBLANES)


def _dispatch(pos_flat, m3):
    groups, _, w = m3.shape
    t = groups * SUBLANES
    tm = ROW_TILE
    return pl.pallas_call(
        _dispatch_kernel,
        out_shape=jax.ShapeDtypeStruct((t * TOP_K, w), m3.dtype),
        grid_spec=pltpu.PrefetchScalarGridSpec(
            num_scalar_prefetch=1,
            grid=(t // tm,),
            in_specs=[pl.BlockSpec((tm // SUBLANES, SUBLANES, w), lambda i, pos: (i, 0, 0))],
            out_specs=pl.BlockSpec(memory_space=pl.ANY),
            scratch_shapes=[pltpu.SemaphoreType.DMA],
        ),
        compiler_params=_cparams("arbitrary"),
        name="dispatch",
    )(pos_flat, m3)


def _experts_kernel(tile_ref, exp_ref, lo_ref, hi_ref, first_ref, newexp_ref,
                    x_ref, wg_ref, wu_ref, wd_ref, y_ref, wg_bf, wu_bf, wd_bf):
    i = pl.program_id(0)
    lo, hi = lo_ref[i], hi_ref[i]

    @pl.when(newexp_ref[i] == 1)
    def _():
        wg_bf[...] = wg_ref[0].astype(BF16)
        wu_bf[...] = wu_ref[0].astype(BF16)
        wd_bf[...] = wd_ref[0].astype(BF16)

    @pl.when(hi > lo)
    def _():
        w = x_ref.shape[1]
        x_lo, x_hi = _unpack_halves(x_ref[...])
        x_lo, x_hi = x_lo.astype(BF16), x_hi.astype(BF16)
        hg = (jnp.dot(x_lo, wg_bf[0:w, :], preferred_element_type=F32)
              + jnp.dot(x_hi, wg_bf[w:, :], preferred_element_type=F32))
        hu = (jnp.dot(x_lo, wu_bf[0:w, :], preferred_element_type=F32)
              + jnp.dot(x_hi, wu_bf[w:, :], preferred_element_type=F32))
        hid = (hg * jax.nn.sigmoid(hg)) * hu
        y = _pack_halves(jnp.dot(hid.astype(BF16), wd_bf[...], preferred_element_type=F32))
        row = lax.broadcasted_iota(jnp.int32, (x_ref.shape[0], 1), 0)
        mine = (row >= lo) & (row < hi)

        @pl.when(first_ref[i] == 1)
        def _():
            y_ref[...] = jnp.where(mine, y, jnp.uint32(0))

        @pl.when(first_ref[i] == 0)
        def _():
            y_ref[...] = jnp.where(mine, y, y_ref[...])


def _experts(meta, xs, wg, wu, wd):
    n, w = xs.shape
    tm = EXPERT_TILE
    n_items = meta[0].shape[0]
    d, f = wg.shape[-2:]
    assert d == 2 * w
    idx = lambda fn: (lambda i, tl, ex, lo, hi, fi, ne: fn(i, tl, ex))
    return pl.pallas_call(
        _experts_kernel,
        out_shape=jax.ShapeDtypeStruct((n, w), xs.dtype),
        grid_spec=pltpu.PrefetchScalarGridSpec(
            num_scalar_prefetch=len(meta),
            grid=(n_items,),
            in_specs=[
                pl.BlockSpec((tm, w), idx(lambda i, tl, ex: (tl[i], 0))),
                pl.BlockSpec((1, d, f), idx(lambda i, tl, ex: (ex[i], 0, 0))),
                pl.BlockSpec((1, d, f), idx(lambda i, tl, ex: (ex[i], 0, 0))),
                pl.BlockSpec((1, f, d), idx(lambda i, tl, ex: (ex[i], 0, 0))),
            ],
            out_specs=pl.BlockSpec((tm, w), idx(lambda i, tl, ex: (tl[i], 0))),
            scratch_shapes=[pltpu.VMEM((d, f), BF16), pltpu.VMEM((d, f), BF16), pltpu.VMEM((f, d), BF16)],
        ),
        compiler_params=_cparams("arbitrary"),
        name="experts",
    )(*meta, xs, wg, wu, wd)


def _expert_work_items(counts, n_rows):
    tm = EXPERT_TILE
    n_tiles = n_rows // tm
    n_items = n_tiles + N_EXPERTS - 1
    offs = jnp.cumsum(counts) - counts
    ends = offs + counts
    t_first = offs // tm
    t_last = jnp.where(counts > 0, (ends - 1) // tm, t_first - 1)
    per_e = t_last - t_first + 1
    item_end = jnp.cumsum(per_e)
    item_start = item_end - per_e
    total = item_end[-1]
    ids = jnp.arange(n_items, dtype=jnp.int32)
    e = jnp.minimum(jnp.sum(item_end[None, :] <= ids[:, None], axis=1), N_EXPERTS - 1).astype(jnp.int32)
    tile = t_first[e] + (ids - item_start[e])
    lo = jnp.clip(offs[e] - tile * tm, 0, tm)
    hi = jnp.clip(ends[e] - tile * tm, 0, tm)
    valid = ids < total
    last = jnp.maximum(total - 1, 0)
    tile = jnp.where(valid, tile, tile[last])
    e = jnp.where(valid, e, e[last])
    lo = jnp.where(valid, lo, 0)
    hi = jnp.where(valid, hi, 0)
    prev_tile = jnp.concatenate([jnp.full((1,), -1, tile.dtype), tile[:-1]])
    first = (valid & (tile != prev_tile)).astype(jnp.int32)
    prev_e = jnp.concatenate([jnp.full((1,), -1, e.dtype), e[:-1]])
    new_expert = e != prev_e
    return tuple(a.astype(jnp.int32) for a in (tile, e, lo, hi, first, new_expert))


def _final_kernel(pos_ref, h1_ref, route_ref, p_ref, wple_ref, wpg_ref, gple_ref, gfin_ref, ys_ref,
                  o_ref, ybuf_ref, h2_ref, sem):
    tm = h1_ref.shape[0]
    groups = tm // SUBLANES
    i = pl.program_id(0)
    n = pl.num_programs(0)
    slot = i % 2

    def start_group(tile, slot_, g):
        base = tile * (tm * TOP_K)
        for j in range(SUBLANES):
            for k in range(TOP_K):
                src = pos_ref[base + TOP_K * (g * SUBLANES + j) + k]
                pltpu.make_async_copy(ys_ref.at[pl.ds(src, 1)], ybuf_ref.at[slot_, k, g, pl.ds(j, 1)],
                                      sem.at[slot_]).start()

    def wait_tile(slot_):
        for _ in range(tm * TOP_K):
            pltpu.make_async_copy(ys_ref.at[pl.ds(0, 1)], ybuf_ref.at[slot_, 0, 0, pl.ds(0, 1)],
                                  sem.at[slot_]).wait()

    @pl.when(i == 0)
    def _():
        def body(g, c):
            start_group(0, 0, g)
            return c
        lax.fori_loop(0, groups, body, 0, unroll=DMA_UNROLL // SUBLANES)

    wait_tile(slot)

    lane = lax.broadcasted_iota(jnp.int32, (tm, LANES), 1)
    route = route_ref[...]
    w0 = jnp.sum(jnp.where(lane == 4, route, 0.0), axis=-1, keepdims=True)
    w1 = jnp.sum(jnp.where(lane == 5, route, 0.0), axis=-1, keepdims=True)
    w = ybuf_ref.shape[-1]
    y0_lo, y0_hi = _unpack_halves(ybuf_ref[slot, 0].reshape(tm, w))
    y1_lo, y1_hi = _unpack_halves(ybuf_ref[slot, 1].reshape(tm, w))
    moe = jnp.concatenate([w0 * y0_lo + w1 * y1_lo, w0 * y0_hi + w1 * y1_hi], axis=1)
    h2_ref[...] = h1_ref[...] + moe

    nxt = jnp.minimum(i + 1, n - 1)
    for g in range(groups):
        start_group(nxt, 1 - slot, g)

    h2 = h2_ref[...]
    nrm = _rms(h2, gple_ref[...]).astype(BF16)
    gate = jax.nn.sigmoid(jnp.dot(nrm, wpg_ref[...], preferred_element_type=F32))
    pw = jnp.dot(p_ref[...].astype(BF16), wple_ref[...], preferred_element_type=F32)
    h3 = h2 + gate * pw
    o_ref[...] = _rms(h3, gfin_ref[...])

    @pl.when(i == n - 1)
    def _():
        wait_tile(1 - slot)


def _final(pos_flat, h1, route, p2, wple_bf16, wpg_bf16, gple, gfin, ys):
    t, d = h1.shape
    tm = ROW_TILE
    row_spec = lambda w: pl.BlockSpec((tm, w), lambda i, pos: (i, 0))
    const = lambda shape: pl.BlockSpec(shape, lambda i, pos: (0, 0))
    return pl.pallas_call(
        _final_kernel,
        out_shape=jax.ShapeDtypeStruct((t, d), F32),
        grid_spec=pltpu.PrefetchScalarGridSpec(
            num_scalar_prefetch=1,
            grid=(t // tm,),
            in_specs=[
                row_spec(d), row_spec(LANES), row_spec(PLE_DIM),
                const((PLE_DIM, d)),
                pl.BlockSpec((d, d), lambda i, pos: (0, 0), pipeline_mode=pl.Buffered(1)),
                const((1, d)), const((1, d)),
                pl.BlockSpec(memory_space=pl.ANY),
            ],
            out_specs=row_spec(d),
            scratch_shapes=[
                pltpu.VMEM((2, TOP_K, tm // SUBLANES, SUBLANES, ys.shape[1]), ys.dtype),
                pltpu.VMEM((tm, d), F32),
                pltpu.SemaphoreType.DMA((2,)),
            ],
        ),
        compiler_params=_cparams("arbitrary"),
        name="final",
    )(pos_flat, h1, route, p2, wple_bf16, wpg_bf16, gple, gfin, ys)


def _router_weights(w_grp, b_grp, w_exp, b_exp):
    d = w_grp.shape[0]
    pad = LANES - N_EXPERTS - N_GROUPS
    wr = jnp.concatenate([w_exp.reshape(d, N_EXPERTS), w_grp, jnp.zeros((d, pad), F32)], axis=1)
    br = jnp.concatenate([b_exp.reshape(N_EXPERTS), b_grp, jnp.zeros((pad,), F32)]).reshape(1, LANES)
    return wr.astype(BF16), br.astype(F32)


def kernel(x, p, w_in, w_out, sinks, g_mix, g_moe, g_ple, g_final, w_grp, b_grp, w_exp, b_exp,
           w_gate, w_up, w_down, w_ple, w_ple_gate):
    b, s, d = x.shape
    t = b * s
    depth = w_in.shape[0]
    assert d == D_MODEL and t % ROW_TILE == 0 and (t * TOP_K) % EXPERT_TILE == 0
    assert depth == 1, "the final rmsnorm is fused into the (single) layer's last kernel"
    slopes_a = _alibi_slopes(A_HEADS)
    slopes_b = _alibi_slopes(B_HEADS)

    h = x.reshape(t, d)
    for i in range(depth):
        proj = _proj(h, g_mix[i].reshape(1, d), w_in[i].astype(BF16))
        proj3 = proj.reshape(b, s, IN_COLS)
        mixed_a = _attn_a(proj3, slopes_a).reshape(t, A_WIDTH)
        mixed_b = _attn_b(proj3, slopes_b, sinks[i]).reshape(t, B_WIDTH)

        wr, br = _router_weights(w_grp[i], b_grp[i], w_exp[i], b_exp[i])
        h1, m, route, cnt = _outproj(h, mixed_a, mixed_b, w_out[i].astype(BF16), g_moe[i].reshape(1, d), wr, br)

        counts = cnt[0, :N_EXPERTS].astype(jnp.int32)
        offs = jnp.cumsum(counts) - counts
        eid = route[:, 0:TOP_K].astype(jnp.int32)
        rank = route[:, TOP_K:2 * TOP_K].astype(jnp.int32)
        pos_flat = (offs[eid] + rank).reshape(t * TOP_K)

        xs = _dispatch(pos_flat, m)
        ys = _experts(_expert_work_items(counts, t * TOP_K), xs, w_gate[i], w_up[i], w_down[i])
        h_next = _final(pos_flat, h1, route, p[i].reshape(t, PLE_DIM), w_ple[i].astype(BF16),
                        w_ple_gate[i].astype(BF16), g_ple[i].reshape(1, d), g_final.reshape(1, d), ys)
        h = h_next
    return h.reshape(b, s, d)
```

```python
import functools

import numpy as np
import jax
import jax.numpy as jnp
from jax import lax
from jax.experimental import pallas as pl
from jax.experimental.pallas import tpu as pltpu

D_MODEL = 2048
PLE_DIM = 256
BLK = 128
EPS = 1e-6
A_HEAD_DIM = 128
A_WIDTH = D_MODEL // 2
A_HEADS = A_WIDTH // A_HEAD_DIM
A_BRANCHES = ((128, 1), (512, 4), (2048, 16))
B_HEAD_DIM = 64
B_WIDTH = D_MODEL - A_WIDTH
B_HEADS = B_WIDTH // B_HEAD_DIM
B_GROUP = 8
B_KV_HEADS = B_HEADS // B_GROUP
B_WINDOW = 128
IN_COLS = 3 * A_WIDTH + B_WIDTH + 2 * B_KV_HEADS * B_HEAD_DIM
N_GROUPS = 4
EXPERTS_PER_GROUP = 8
N_EXPERTS = N_GROUPS * EXPERTS_PER_GROUP
TOP_K = 2
D_EXPERT = D_MODEL // 8

LANES = 128
SUBLANES = 8
V7X_VMEM_LIMIT_BYTES = 56 * 1024 * 1024

ROW_TILE = 512
EXPERT_TILE = 256
DMA_UNROLL = 16

F32 = jnp.float32
BF16 = jnp.bfloat16
NEG_INF = float("-inf")
LOG2E = 1.4426950408889634


def _cparams(*sem):
    return pltpu.CompilerParams(dimension_semantics=sem, vmem_limit_bytes=V7X_VMEM_LIMIT_BYTES)


def _rms(x, g):
    return x * lax.rsqrt(jnp.mean(x * x, axis=-1, keepdims=True) + EPS) * g


def _alibi_slopes(n):
    return jnp.asarray(np.array([2.0 ** (-8.0 * (i + 1) / n) for i in range(n)], dtype=np.float32))


def _proj_kernel(x_ref, g_ref, w_ref, o_ref):
    a = _rms(x_ref[...], g_ref[...]).astype(BF16)
    o_ref[...] = jnp.dot(a, w_ref[...], preferred_element_type=F32)


def _proj(x2, g, w_bf16):
    t, d = x2.shape
    n = w_bf16.shape[1]
    return pl.pallas_call(
        _proj_kernel,
        out_shape=jax.ShapeDtypeStruct((t, n), F32),
        grid=(t // ROW_TILE,),
        in_specs=[
            pl.BlockSpec((ROW_TILE, d), lambda i: (i, 0)),
            pl.BlockSpec((1, d), lambda i: (0, 0)),
            pl.BlockSpec((d, n), lambda i: (0, 0), pipeline_mode=pl.Buffered(1)),
        ],
        out_specs=pl.BlockSpec((ROW_TILE, n), lambda i: (i, 0)),
        compiler_params=_cparams("arbitrary"),
        name="proj",
    )(x2, g, w_bf16)


def _software_pipeline(items, stages):
    state = [None] * len(items)
    for t in range(len(items) + len(stages) - 1):
        for s, stage in enumerate(stages):
            j = t - s
            if 0 <= j < len(items):
                state[j] = stage(items[j], state[j])


def _nt_dot(a, b):
    return lax.dot_general(a, b, (((1,), (1,)), ((), ())), preferred_element_type=F32)


def _attn_a_kernel(slopes_ref, q_ref, k_ref, v_ref, o_ref, acc_ref, m_ref, den_ref):
    seq = q_ref.shape[1]
    slope = slopes_ref[pl.program_id(1)]
    scale2 = A_HEAD_DIM ** -0.5 * LOG2E
    qi = lax.broadcasted_iota(jnp.int32, (BLK, BLK), 0)
    kj = lax.broadcasted_iota(jnp.int32, (BLK, BLK), 1)
    d_cur = qi - kj
    d_prev = d_cur + BLK

    items = []
    for br, (window, dil) in enumerate(A_BRANCHES):
        max_delta = window // dil
        assert max_delta <= BLK and seq % (dil * BLK) == 0
        bias_cur = jnp.where((d_cur >= 0) & (d_cur <= max_delta),
                             -(slope * (dil * d_cur).astype(F32)) * LOG2E, NEG_INF)
        bias_prev = jnp.where((d_prev >= 0) & (d_prev <= max_delta),
                              -(slope * (dil * d_prev).astype(F32)) * LOG2E, NEG_INF)
        bias_both = jnp.concatenate([bias_prev, bias_cur], axis=1)
        for bi in range(seq // (dil * BLK)):
            for r in range(dil):
                items.append(dict(br=br, dil=dil, start=bi * dil * BLK + r, has_prev=bi > 0,
                                  bias=bias_both if bi > 0 else bias_cur))

    def rows(it, start):
        return pl.ds(start, BLK, stride=it["dil"]) if it["dil"] > 1 else pl.ds(start, BLK)

    def keys_or_values(ref, it):
        cur = ref[0, rows(it, it["start"]), :].astype(BF16)
        if not it["has_prev"]:
            return cur
        prev = ref[0, rows(it, it["start"] - it["dil"] * BLK), :].astype(BF16)
        return jnp.concatenate([prev, cur], axis=0)

    def scores(it, _):
        q = q_ref[0, rows(it, it["start"]), :].astype(BF16)
        return _nt_dot(q, keys_or_values(k_ref, it)) * scale2 + it["bias"]

    def softmax(it, s):
        m = jnp.max(s, axis=-1, keepdims=True)
        return m, jnp.exp2(s - m).astype(BF16)

    def values(it, m_p):
        m, p = m_p
        v = keys_or_values(v_ref, it)
        accden = jnp.dot(p, jnp.concatenate([v, jnp.ones_like(v)], axis=1), preferred_element_type=F32)
        r = rows(it, it["start"])
        acc_ref[it["br"], r, :] = accden[:, :LANES]
        den_ref[it["br"], r, :] = accden[:, LANES:]
        m_ref[it["br"], r, :] = jnp.broadcast_to(m, (BLK, LANES))

    _software_pipeline(items, (scores, softmax, values))

    chunk = 256
    def mix(ci, carry):
        r = pl.ds(pl.multiple_of(ci * chunk, chunk), chunk)
        m0, m1, m2 = m_ref[0, r, :], m_ref[1, r, :], m_ref[2, r, :]
        mx = jnp.maximum(jnp.maximum(m0, m1), m2)
        a0, a1, a2 = jnp.exp2(m0 - mx), jnp.exp2(m1 - mx), jnp.exp2(m2 - mx)
        num = a0 * acc_ref[0, r, :] + a1 * acc_ref[1, r, :] + a2 * acc_ref[2, r, :]
        tot = a0 * den_ref[0, r, :] + a1 * den_ref[1, r, :] + a2 * den_ref[2, r, :]
        o_ref[0, r, :] = (num / tot).astype(o_ref.dtype)
        return carry
    lax.fori_loop(0, seq // chunk, mix, 0)


def _attn_a(proj3, slopes):
    b, s, _ = proj3.shape
    nh = A_HEADS
    return pl.pallas_call(
        _attn_a_kernel,
        out_shape=jax.ShapeDtypeStruct((b, s, A_WIDTH), BF16),
        grid=(b, nh),
        in_specs=[
            pl.BlockSpec(memory_space=pltpu.SMEM),
            pl.BlockSpec((1, s, A_HEAD_DIM), lambda bi, h: (bi, 0, h)),
            pl.BlockSpec((1, s, A_HEAD_DIM), lambda bi, h: (bi, 0, nh + h)),
            pl.BlockSpec((1, s, A_HEAD_DIM), lambda bi, h: (bi, 0, 2 * nh + h)),
        ],
        out_specs=pl.BlockSpec((1, s, A_HEAD_DIM), lambda bi, h: (bi, 0, h)),
        scratch_shapes=[pltpu.VMEM((len(A_BRANCHES), s, LANES), F32)] * 3,
        compiler_params=_cparams("arbitrary", "arbitrary"),
        name="attn_a",
    )(slopes, proj3, proj3, proj3)


PAIRS_PER_KV = B_GROUP * B_HEAD_DIM // LANES


def _attn_b_kernel(slopes_ref, sinks_ref, q_ref, k_ref, v_ref, o_ref, k2_ref, v2_ref):
    seq = q_ref.shape[1]
    n_blk = seq // BLK
    pair = pl.program_id(1)
    kv_is0 = (pair // PAIRS_PER_KV) == 0
    scale = B_HEAD_DIM ** -0.5
    max_delta = B_WINDOW - 1
    lane = lax.broadcasted_iota(jnp.int32, (BLK, LANES), 1)
    lo_half = lane < B_HEAD_DIM

    row2 = lax.broadcasted_iota(jnp.int32, (2 * BLK, LANES), 0)
    lane2 = lax.broadcasted_iota(jnp.int32, (2 * BLK, LANES), 1)
    head_ones = jnp.where((row2 < BLK) == (lane2 < B_HEAD_DIM), 1.0, 0.0).astype(BF16)

    @pl.when(pair % PAIRS_PER_KV == 0)
    def _():
        def stage(bi, carry):
            r = pl.ds(pl.multiple_of(bi * BLK, BLK), BLK)
            r2 = pl.ds(pl.multiple_of(bi * 2 * BLK, 2 * BLK), 2 * BLK)
            staged = []
            for src in (k_ref, v_ref):
                t = src[0, r, :]
                t_rot = pltpu.roll(t, B_HEAD_DIM, axis=1)
                in_lo = jnp.where(kv_is0, t, t_rot)
                in_hi = jnp.where(kv_is0, t_rot, t)
                staged.append(jnp.concatenate([jnp.where(lo_half, in_lo, 0.0),
                                               jnp.where(lo_half, 0.0, in_hi)], axis=0).astype(BF16))
            k2_ref[r2, :] = staged[0]
            v2_ref[r2, :] = jnp.concatenate([staged[1], head_ones], axis=1)
            return carry
        lax.fori_loop(0, n_blk, stage, 0)

    qi = lax.broadcasted_iota(jnp.int32, (BLK, BLK), 0)
    kj = lax.broadcasted_iota(jnp.int32, (BLK, BLK), 1)
    d_cur = qi - kj
    d_prev = d_cur + BLK
    assert max_delta == BLK - 1
    in_prev = kj > qi
    heads = (2 * pair, 2 * pair + 1)
    bias_fold, bias_first, sink = [], [], []
    for hq in heads:
        slope = slopes_ref[hq]
        b_cur = -(slope * d_cur.astype(F32))
        bias_fold.append(jnp.where(in_prev, -(slope * d_prev.astype(F32)), b_cur))
        bias_first.append(jnp.where(in_prev, NEG_INF, b_cur))
        sink.append(sinks_ref[hq])

    def staged_rows(bi):
        first = max(bi - 1, 0)
        return pl.ds(first * 2 * BLK, (bi + 1 - first) * 2 * BLK)

    def scores(bi, _):
        q = (q_ref[0, pl.ds(bi * BLK, BLK), :] * scale).astype(BF16)
        return _nt_dot(q, k2_ref[staged_rows(bi), :])

    def softmax(bi, s_all):
        has_prev = bi > 0
        p_c, p_p, sink_term = [], [], []
        for j in range(2):
            if has_prev:
                s_p = s_all[:, j * BLK:(j + 1) * BLK]
                s_c = s_all[:, (2 + j) * BLK:(3 + j) * BLK]
                s = jnp.where(in_prev, s_p, s_c) + bias_fold[j]
            else:
                s = s_all[:, j * BLK:(j + 1) * BLK] + bias_first[j]
            m = jnp.maximum(jnp.max(s, axis=-1, keepdims=True), sink[j])
            p = jnp.exp(s - m)
            sink_term.append(jnp.exp(sink[j] - m))
            if has_prev:
                pp = jnp.where(in_prev, p, 0.0)
                p_p.append(pp.astype(BF16))
                p_c.append((p - pp).astype(BF16))
            else:
                p_c.append(p.astype(BF16))
        return jnp.concatenate(p_p + p_c, axis=1), jnp.where(lo_half, sink_term[0], sink_term[1])

    def values(bi, p_sink):
        p, sink_den = p_sink
        accden = jnp.dot(p, v2_ref[staged_rows(bi), :], preferred_element_type=F32)
        o_ref[0, pl.ds(bi * BLK, BLK), :] = (accden[:, :LANES] / (accden[:, LANES:] + sink_den)).astype(o_ref.dtype)

    _software_pipeline(list(range(n_blk)), (scores, softmax, values))


def _attn_b(proj3, slopes, sinks):
    b, s, _ = proj3.shape
    q0 = 3 * A_WIDTH // LANES
    kblk = (3 * A_WIDTH + B_WIDTH) // LANES
    assert B_KV_HEADS * B_HEAD_DIM == LANES
    return pl.pallas_call(
        _attn_b_kernel,
        out_shape=jax.ShapeDtypeStruct((b, s, B_WIDTH), BF16),
        grid=(b, B_WIDTH // LANES),
        in_specs=[
            pl.BlockSpec(memory_space=pltpu.SMEM),
            pl.BlockSpec(memory_space=pltpu.SMEM),
            pl.BlockSpec((1, s, LANES), lambda bi, p: (bi, 0, q0 + p)),
            pl.BlockSpec((1, s, LANES), lambda bi, p: (bi, 0, kblk)),
            pl.BlockSpec((1, s, LANES), lambda bi, p: (bi, 0, kblk + 1)),
        ],
        out_specs=pl.BlockSpec((1, s, LANES), lambda bi, p: (bi, 0, p)),
        scratch_shapes=[
            pltpu.VMEM((2 * s, LANES), BF16),
            pltpu.VMEM((2 * s, 2 * LANES), BF16),
        ],
        compiler_params=_cparams("arbitrary", "arbitrary"),
        name="attn_b",
    )(slopes, sinks, proj3, proj3, proj3)


GRP_LANE0 = N_EXPERTS
ROUTE_FIELDS = 6


def _outproj_kernel(x_ref, ma_ref, mb_ref, wo_ref, g_ref, wr_ref, br_ref,
                    h1_ref, m_ref, route_ref, route_t_ref, cnt_ref, carry_ref):
    tm = x_ref.shape[0]

    @pl.when(pl.program_id(0) == 0)
    def _():
        carry_ref[...] = jnp.zeros_like(carry_ref)

    h1 = (x_ref[...]
          + jnp.dot(ma_ref[...], wo_ref[0:A_WIDTH, :], preferred_element_type=F32)
          + jnp.dot(mb_ref[...], wo_ref[A_WIDTH:, :], preferred_element_type=F32))
    h1_ref[...] = h1
    m = _rms(h1, g_ref[...])
    m_ref[...] = m.reshape(m_ref.shape)
    logits = jnp.dot(m.astype(BF16), wr_ref[...], preferred_element_type=F32) + br_ref[...]

    lane = lax.broadcasted_iota(jnp.int32, (tm, LANES), 1).astype(F32)
    big = float(LANES)
    gl = jnp.where((lane >= GRP_LANE0) & (lane < GRP_LANE0 + N_GROUPS), logits, NEG_INF)
    gmax = jnp.max(gl, axis=-1, keepdims=True)
    gidx = jnp.min(jnp.where(gl == gmax, lane, big), axis=-1, keepdims=True) - GRP_LANE0
    grp_w = 1.0 / jnp.sum(jnp.exp(gl - gmax), axis=-1, keepdims=True)
    e_lo = gidx * EXPERTS_PER_GROUP
    el = jnp.where((lane >= e_lo) & (lane < e_lo + EXPERTS_PER_GROUP), logits, NEG_INF)
    v1 = jnp.max(el, axis=-1, keepdims=True)
    i1 = jnp.min(jnp.where(el == v1, lane, big), axis=-1, keepdims=True)
    el2 = jnp.where(lane == i1, NEG_INF, el)
    v2 = jnp.max(el2, axis=-1, keepdims=True)
    i2 = jnp.min(jnp.where(el2 == v2, lane, big), axis=-1, keepdims=True)
    e2 = jnp.exp(v2 - v1)
    w1 = (1.0 / (1.0 + e2)) * grp_w
    w2 = (e2 / (1.0 + e2)) * grp_w

    onehot = jnp.where(lane == i1, 1.0, 0.0) + jnp.where(lane == i2, 1.0, 0.0)
    row = lax.broadcasted_iota(jnp.int32, (tm, tm), 0)
    col = lax.broadcasted_iota(jnp.int32, (tm, tm), 1)
    earlier = jnp.where(row > col, 1.0, 0.0).astype(BF16)
    before = carry_ref[0:1, :] + jnp.dot(earlier, onehot.astype(BF16), preferred_element_type=F32)
    r1 = jnp.sum(jnp.where(lane == i1, before, 0.0), axis=-1, keepdims=True)
    r2 = jnp.sum(jnp.where(lane == i2, before, 0.0), axis=-1, keepdims=True)
    carry_ref[...] = carry_ref[...] + jnp.sum(onehot, axis=0, keepdims=True)
    cnt_ref[...] = carry_ref[...]

    route = jnp.zeros((tm, LANES), F32)
    for idx, val in enumerate((i1, i2, r1, r2, w1, w2)):
        route = jnp.where(lane == float(idx), val, route)
    route_ref[...] = route
    route_t_ref[...] = jnp.transpose(route)[0:SUBLANES, :]


def _outproj(x2, ma, mb, wo_bf16, g, wr_bf16, br):
    t, d = x2.shape
    tm = ROW_TILE
    row_spec = lambda w: pl.BlockSpec((tm, w), lambda i: (i, 0))
    const = lambda shape: pl.BlockSpec(shape, lambda i: (0, 0))
    return pl.pallas_call(
        _outproj_kernel,
        out_shape=(
            jax.ShapeDtypeStruct((t, d), F32),
            jax.ShapeDtypeStruct((t // SUBLANES, SUBLANES, d), F32),
            jax.ShapeDtypeStruct((t, LANES), F32),
            jax.ShapeDtypeStruct((SUBLANES, t), F32),
            jax.ShapeDtypeStruct((8, LANES), F32),
        ),
        grid=(t // tm,),
        in_specs=[
            row_spec(d), row_spec(A_WIDTH), row_spec(B_WIDTH),
            pl.BlockSpec((d, d), lambda i: (0, 0), pipeline_mode=pl.Buffered(1)),
            const((1, d)), const((d, LANES)), const((1, LANES)),
        ],
        out_specs=(row_spec(d),
                   pl.BlockSpec((tm // SUBLANES, SUBLANES, d), lambda i: (i, 0, 0)),
                   row_spec(LANES), pl.BlockSpec((SUBLANES, tm), lambda i: (0, i)), const((8, LANES))),
        scratch_shapes=[pltpu.VMEM((8, LANES), F32)],
        compiler_params=_cparams("arbitrary"),
        name="outproj",
    )(x2, ma, mb, wo_bf16, g, wr_bf16, br)


def _dispatch_kernel(pos_ref, m_ref, xs_ref, sem):
    groups = m_ref.shape[0]
    base = pl.program_id(0) * (groups * SUBLANES * TOP_K)

    def start(g, c):
        for j in range(SUBLANES):
            for k in range(TOP_K):
                dst = pos_ref[base + TOP_K * (g * SUBLANES + j) + k]
                pltpu.make_async_copy(m_ref.at[g, pl.ds(j, 1)], xs_ref.at[pl.ds(dst, 1)], sem).start()
        return c
    lax.fori_loop(0, groups, start, 0, unroll=DMA_UNROLL // SUBLANES)

    def wait(g, c):
        for _ in range(SUBLANES * TOP_K):
            pltpu.make_async_copy(m_ref.at[0, pl.ds(0, 1)], xs_ref.at[pl.ds(0, 1)], sem).wait()
        return c
    lax.fori_loop(0, groups, wait, 0, unroll=DMA_UNROLL // SUBLANES)


def _dispatch(pos_flat, m3):
    groups, _, w = m3.shape
    t = groups * SUBLANES
    tm = ROW_TILE
    return pl.pallas_call(
        _dispatch_kernel,
        out_shape=jax.ShapeDtypeStruct((t * TOP_K, w), m3.dtype),
        grid_spec=pltpu.PrefetchScalarGridSpec(
            num_scalar_prefetch=1,
            grid=(t // tm,),
            in_specs=[pl.BlockSpec((tm // SUBLANES, SUBLANES, w), lambda i, pos: (i, 0, 0))],
            out_specs=pl.BlockSpec(memory_space=pl.ANY),
            scratch_shapes=[pltpu.SemaphoreType.DMA],
        ),
        compiler_params=_cparams("arbitrary"),
        name="dispatch",
    )(pos_flat, m3)


def _experts_kernel(tile_ref, exp_ref, lo_ref, hi_ref, first_ref, newexp_ref,
                    x_ref, wg_ref, wu_ref, wd_ref, y_ref, wg_bf, wu_bf, wd_bf):
    i = pl.program_id(0)
    lo, hi = lo_ref[i], hi_ref[i]

    @pl.when(newexp_ref[i] == 1)
    def _():
        wg_bf[...] = wg_ref[0].astype(BF16)
        wu_bf[...] = wu_ref[0].astype(BF16)
        wd_bf[...] = wd_ref[0].astype(BF16)

    @pl.when(hi > lo)
    def _():
        x = x_ref[...].astype(BF16)
        hg = jnp.dot(x, wg_bf[...], preferred_element_type=F32)
        hu = jnp.dot(x, wu_bf[...], preferred_element_type=F32)
        hid = (hg * jax.nn.sigmoid(hg)) * hu
        y = jnp.dot(hid.astype(BF16), wd_bf[...], preferred_element_type=F32)
        row = lax.broadcasted_iota(jnp.int32, (x_ref.shape[0], 1), 0)
        mine = (row >= lo) & (row < hi)

        @pl.when(first_ref[i] == 1)
        def _():
            y_ref[...] = jnp.where(mine, y, 0.0)

        @pl.when(first_ref[i] == 0)
        def _():
            y_ref[...] = jnp.where(mine, y, y_ref[...])


def _experts(meta, xs, wg, wu, wd):
    n, w = xs.shape
    tm = EXPERT_TILE
    n_items = meta[0].shape[0]
    d, f = wg.shape[-2:]
    assert d == w
    idx = lambda fn: (lambda i, tl, ex, lo, hi, fi, ne: fn(i, tl, ex))
    return pl.pallas_call(
        _experts_kernel,
        out_shape=jax.ShapeDtypeStruct((n, w), xs.dtype),
        grid_spec=pltpu.PrefetchScalarGridSpec(
            num_scalar_prefetch=len(meta),
            grid=(n_items,),
            in_specs=[
                pl.BlockSpec((tm, w), idx(lambda i, tl, ex: (tl[i], 0))),
                pl.BlockSpec((1, d, f), idx(lambda i, tl, ex: (ex[i], 0, 0))),
                pl.BlockSpec((1, d, f), idx(lambda i, tl, ex: (ex[i], 0, 0))),
                pl.BlockSpec((1, f, d), idx(lambda i, tl, ex: (ex[i], 0, 0))),
            ],
            out_specs=pl.BlockSpec((tm, w), idx(lambda i, tl, ex: (tl[i], 0))),
            scratch_shapes=[pltpu.VMEM((d, f), BF16), pltpu.VMEM((d, f), BF16), pltpu.VMEM((f, d), BF16)],
        ),
        compiler_params=_cparams("arbitrary"),
        name="experts",
    )(*meta, xs, wg, wu, wd)


def _expert_work_items(counts, n_rows):
    tm = EXPERT_TILE
    n_tiles = n_rows // tm
    n_items = n_tiles + N_EXPERTS - 1
    offs = jnp.cumsum(counts) - counts
    ends = offs + counts
    t_first = offs // tm
    t_last = jnp.where(counts > 0, (ends - 1) // tm, t_first - 1)
    per_e = t_last - t_first + 1
    item_end = jnp.cumsum(per_e)
    item_start = item_end - per_e
    total = item_end[-1]
    ids = jnp.arange(n_items, dtype=jnp.int32)
    e = jnp.minimum(jnp.sum(item_end[None, :] <= ids[:, None], axis=1), N_EXPERTS - 1).astype(jnp.int32)
    tile = t_first[e] + (ids - item_start[e])
    lo = jnp.clip(offs[e] - tile * tm, 0, tm)
    hi = jnp.clip(ends[e] - tile * tm, 0, tm)
    valid = ids < total
    last = jnp.maximum(total - 1, 0)
    tile = jnp.where(valid, tile, tile[last])
    e = jnp.where(valid, e, e[last])
    lo = jnp.where(valid, lo, 0)
    hi = jnp.where(valid, hi, 0)
    prev_tile = jnp.concatenate([jnp.full((1,), -1, tile.dtype), tile[:-1]])
    first = (valid & (tile != prev_tile)).astype(jnp.int32)
    prev_e = jnp.concatenate([jnp.full((1,), -1, e.dtype), e[:-1]])
    new_expert = e != prev_e
    return tuple(a.astype(jnp.int32) for a in (tile, e, lo, hi, first, new_expert))


def _final_kernel(pos_ref, h1_ref, route_ref, p_ref, wple_ref, wpg_ref, gple_ref, gfin_ref, ys_ref,
                  o_ref, ybuf_ref, h2_ref, sem):
    tm = h1_ref.shape[0]
    groups = tm // SUBLANES
    i = pl.program_id(0)
    n = pl.num_programs(0)
    slot = i % 2

    def start_group(tile, slot_, g):
        base = tile * (tm * TOP_K)
        for j in range(SUBLANES):
            for k in range(TOP_K):
                src = pos_ref[base + TOP_K * (g * SUBLANES + j) + k]
                pltpu.make_async_copy(ys_ref.at[pl.ds(src, 1)], ybuf_ref.at[slot_, k, g, pl.ds(j, 1)],
                                      sem.at[slot_]).start()

    def wait_tile(slot_):
        for _ in range(tm * TOP_K):
            pltpu.make_async_copy(ys_ref.at[pl.ds(0, 1)], ybuf_ref.at[slot_, 0, 0, pl.ds(0, 1)],
                                  sem.at[slot_]).wait()

    @pl.when(i == 0)
    def _():
        def body(g, c):
            start_group(0, 0, g)
            return c
        lax.fori_loop(0, groups, body, 0, unroll=DMA_UNROLL // SUBLANES)

    wait_tile(slot)

    lane = lax.broadcasted_iota(jnp.int32, (tm, LANES), 1)
    route = route_ref[...]
    w0 = jnp.sum(jnp.where(lane == 4, route, 0.0), axis=-1, keepdims=True)
    w1 = jnp.sum(jnp.where(lane == 5, route, 0.0), axis=-1, keepdims=True)
    d = ybuf_ref.shape[-1]
    moe = w0 * ybuf_ref[slot, 0].reshape(tm, d) + w1 * ybuf_ref[slot, 1].reshape(tm, d)
    h2_ref[...] = h1_ref[...] + moe

    nxt = jnp.minimum(i + 1, n - 1)
    for g in range(groups):
        start_group(nxt, 1 - slot, g)

    h2 = h2_ref[...]
    nrm = _rms(h2, gple_ref[...]).astype(BF16)
    gate = jax.nn.sigmoid(jnp.dot(nrm, wpg_ref[...], preferred_element_type=F32))
    pw = jnp.dot(p_ref[...].astype(BF16), wple_ref[...], preferred_element_type=F32)
    h3 = h2 + gate * pw
    o_ref[...] = _rms(h3, gfin_ref[...])

    @pl.when(i == n - 1)
    def _():
        wait_tile(1 - slot)


def _final(pos_flat, h1, route, p2, wple_bf16, wpg_bf16, gple, gfin, ys):
    t, d = h1.shape
    tm = ROW_TILE
    row_spec = lambda w: pl.BlockSpec((tm, w), lambda i, pos: (i, 0))
    const = lambda shape: pl.BlockSpec(shape, lambda i, pos: (0, 0))
    return pl.pallas_call(
        _final_kernel,
        out_shape=jax.ShapeDtypeStruct((t, d), F32),
        grid_spec=pltpu.PrefetchScalarGridSpec(
            num_scalar_prefetch=1,
            grid=(t // tm,),
            in_specs=[
                row_spec(d), row_spec(LANES), row_spec(PLE_DIM),
                const((PLE_DIM, d)),
                pl.BlockSpec((d, d), lambda i, pos: (0, 0), pipeline_mode=pl.Buffered(1)),
                const((1, d)), const((1, d)),
                pl.BlockSpec(memory_space=pl.ANY),
            ],
            out_specs=row_spec(d),
            scratch_shapes=[
                pltpu.VMEM((2, TOP_K, tm // SUBLANES, SUBLANES, ys.shape[1]), ys.dtype),
                pltpu.VMEM((tm, d), F32),
                pltpu.SemaphoreType.DMA((2,)),
            ],
        ),
        compiler_params=_cparams("arbitrary"),
        name="final",
    )(pos_flat, h1, route, p2, wple_bf16, wpg_bf16, gple, gfin, ys)


def _router_weights(w_grp, b_grp, w_exp, b_exp):
    d = w_grp.shape[0]
    pad = LANES - N_EXPERTS - N_GROUPS
    wr = jnp.concatenate([w_exp.reshape(d, N_EXPERTS), w_grp, jnp.zeros((d, pad), F32)], axis=1)
    br = jnp.concatenate([b_exp.reshape(N_EXPERTS), b_grp, jnp.zeros((pad,), F32)]).reshape(1, LANES)
    return wr.astype(BF16), br.astype(F32)


def kernel(x, p, w_in, w_out, sinks, g_mix, g_moe, g_ple, g_final, w_grp, b_grp, w_exp, b_exp,
           w_gate, w_up, w_down, w_ple, w_ple_gate):
    b, s, d = x.shape
    t = b * s
    depth = w_in.shape[0]
    assert d == D_MODEL and t % ROW_TILE == 0 and (t * TOP_K) % EXPERT_TILE == 0
    assert depth == 1, "the final rmsnorm is fused into the (single) layer's last kernel"
    slopes_a = _alibi_slopes(A_HEADS)
    slopes_b = _alibi_slopes(B_HEADS)

    h = x.reshape(t, d)
    for i in range(depth):
        proj = _proj(h, g_mix[i].reshape(1, d), w_in[i].astype(BF16))
        proj3 = proj.reshape(b, s, IN_COLS)
        mixed_a = _attn_a(proj3, slopes_a).reshape(t, A_WIDTH)
        mixed_b = _attn_b(proj3, slopes_b, sinks[i]).reshape(t, B_WIDTH)

        wr, br = _router_weights(w_grp[i], b_grp[i], w_exp[i], b_exp[i])
        h1, m, route, route_t, cnt = _outproj(h, mixed_a, mixed_b, w_out[i].astype(BF16),
                                              g_moe[i].reshape(1, d), wr, br)

        counts = cnt[0, :N_EXPERTS].astype(jnp.int32)
        offs = jnp.cumsum(counts) - counts
        eid = route_t[0:TOP_K].astype(jnp.int32)
        rank = route_t[TOP_K:2 * TOP_K].astype(jnp.int32)
        pos_flat = (offs[eid] + rank).T.reshape(t * TOP_K)

        xs = _dispatch(pos_flat, m)
        ys = _experts(_expert_work_items(counts, t * TOP_K), xs, w_gate[i], w_up[i], w_down[i])
        h_next = _final(pos_flat, h1, route, p[i].reshape(t, PLE_DIM), w_ple[i].astype(BF16),
                        w_ple_gate[i].astype(BF16), g_ple[i].reshape(1, d), g_final.reshape(1, d), ys)
        h = h_next
    return h.reshape(b, s, d)
```

```python
import functools

import numpy as np
import jax
import jax.numpy as jnp
from jax import lax
from jax.experimental import pallas as pl
from jax.experimental.pallas import tpu as pltpu

D_MODEL = 2048
PLE_DIM = 256
BLK = 128
EPS = 1e-6
A_HEAD_DIM = 128
A_WIDTH = D_MODEL // 2
A_HEADS = A_WIDTH // A_HEAD_DIM
A_BRANCHES = ((128, 1), (512, 4), (2048, 16))
B_HEAD_DIM = 64
B_WIDTH = D_MODEL - A_WIDTH
B_HEADS = B_WIDTH // B_HEAD_DIM
B_GROUP = 8
B_KV_HEADS = B_HEADS // B_GROUP
B_WINDOW = 128
IN_COLS = 3 * A_WIDTH + B_WIDTH + 2 * B_KV_HEADS * B_HEAD_DIM
N_GROUPS = 4
EXPERTS_PER_GROUP = 8
N_EXPERTS = N_GROUPS * EXPERTS_PER_GROUP
TOP_K = 2
D_EXPERT = D_MODEL // 8

LANES = 128
SUBLANES = 8
V7X_VMEM_LIMIT_BYTES = 56 * 1024 * 1024

ROW_TILE = 512
EXPERT_TILE = 256
DMA_UNROLL = 16
F32 = jnp.float32
BF16 = jnp.bfloat16
NEG_INF = float("-inf")
LOG2E = 1.4426950408889634


def _cparams(*sem):
    return pltpu.CompilerParams(dimension_semantics=sem, vmem_limit_bytes=V7X_VMEM_LIMIT_BYTES)


def _rms(x, g):
    return x * lax.rsqrt(jnp.mean(x * x, axis=-1, keepdims=True) + EPS) * g


def _alibi_slopes(n):
    return jnp.asarray(np.array([2.0 ** (-8.0 * (i + 1) / n) for i in range(n)], dtype=np.float32))


def _proj_kernel(x_ref, g_ref, w_ref, o_ref):
    a = _rms(x_ref[...], g_ref[...]).astype(BF16)
    o_ref[...] = jnp.dot(a, w_ref[...], preferred_element_type=F32)


def _proj(x2, g, w_bf16):
    t, d = x2.shape
    n = w_bf16.shape[1]
    return pl.pallas_call(
        _proj_kernel,
        out_shape=jax.ShapeDtypeStruct((t, n), F32),
        grid=(t // ROW_TILE,),
        in_specs=[
            pl.BlockSpec((ROW_TILE, d), lambda i: (i, 0)),
            pl.BlockSpec((1, d), lambda i: (0, 0)),
            pl.BlockSpec((d, n), lambda i: (0, 0), pipeline_mode=pl.Buffered(1)),
        ],
        out_specs=pl.BlockSpec((ROW_TILE, n), lambda i: (i, 0)),
        compiler_params=_cparams("arbitrary"),
        name="proj",
    )(x2, g, w_bf16)


def _software_pipeline(items, stages):
    state = [None] * len(items)
    for t in range(len(items) + len(stages) - 1):
        for s, stage in enumerate(stages):
            j = t - s
            if 0 <= j < len(items):
                state[j] = stage(items[j], state[j])


def _nt_dot(a, b):
    return lax.dot_general(a, b, (((1,), (1,)), ((), ())), preferred_element_type=F32)


def _attn_a_kernel(slopes_ref, q_ref, k_ref, v_ref, o_ref, acc_ref, m_ref, den_ref):
    seq = q_ref.shape[1]
    slope = slopes_ref[pl.program_id(1)]
    scale2 = A_HEAD_DIM ** -0.5 * LOG2E
    qi = lax.broadcasted_iota(jnp.int32, (BLK, BLK), 0)
    kj = lax.broadcasted_iota(jnp.int32, (BLK, BLK), 1)
    d_cur = qi - kj
    d_prev = d_cur + BLK

    items = []
    for br, (window, dil) in enumerate(A_BRANCHES):
        max_delta = window // dil
        assert max_delta <= BLK and seq % (dil * BLK) == 0
        bias_cur = jnp.where((d_cur >= 0) & (d_cur <= max_delta),
                             -(slope * (dil * d_cur).astype(F32)) * LOG2E, NEG_INF)
        bias_prev = jnp.where((d_prev >= 0) & (d_prev <= max_delta),
                              -(slope * (dil * d_prev).astype(F32)) * LOG2E, NEG_INF)
        bias_both = jnp.concatenate([bias_prev, bias_cur], axis=1)
        for bi in range(seq // (dil * BLK)):
            for r in range(dil):
                items.append(dict(br=br, dil=dil, start=bi * dil * BLK + r, has_prev=bi > 0,
                                  bias=bias_both if bi > 0 else bias_cur))

    def rows(it, start):
        return pl.ds(start, BLK, stride=it["dil"]) if it["dil"] > 1 else pl.ds(start, BLK)

    def keys_or_values(ref, it):
        cur = ref[0, rows(it, it["start"]), :].astype(BF16)
        if not it["has_prev"]:
            return cur
        prev = ref[0, rows(it, it["start"] - it["dil"] * BLK), :].astype(BF16)
        return jnp.concatenate([prev, cur], axis=0)

    def scores(it, _):
        q = q_ref[0, rows(it, it["start"]), :].astype(BF16)
        return _nt_dot(q, keys_or_values(k_ref, it)) * scale2 + it["bias"]

    def softmax(it, s):
        m = jnp.max(s, axis=-1, keepdims=True)
        return m, jnp.exp2(s - m).astype(BF16)

    def values(it, m_p):
        m, p = m_p
        v = keys_or_values(v_ref, it)
        accden = jnp.dot(p, jnp.concatenate([v, jnp.ones_like(v)], axis=1), preferred_element_type=F32)
        r = rows(it, it["start"])
        acc_ref[it["br"], r, :] = accden[:, :LANES]
        den_ref[it["br"], r, :] = accden[:, LANES:]
        m_ref[it["br"], r, :] = jnp.broadcast_to(m, (BLK, LANES))

    _software_pipeline(items, (scores, softmax, values))

    chunk = 256
    def mix(ci, carry):
        r = pl.ds(pl.multiple_of(ci * chunk, chunk), chunk)
        m0, m1, m2 = m_ref[0, r, :], m_ref[1, r, :], m_ref[2, r, :]
        mx = jnp.maximum(jnp.maximum(m0, m1), m2)
        a0, a1, a2 = jnp.exp2(m0 - mx), jnp.exp2(m1 - mx), jnp.exp2(m2 - mx)
        num = a0 * acc_ref[0, r, :] + a1 * acc_ref[1, r, :] + a2 * acc_ref[2, r, :]
        tot = a0 * den_ref[0, r, :] + a1 * den_ref[1, r, :] + a2 * den_ref[2, r, :]
        o_ref[0, r, :] = (num / tot).astype(o_ref.dtype)
        return carry
    lax.fori_loop(0, seq // chunk, mix, 0)


def _attn_a(proj3, slopes):
    b, s, _ = proj3.shape
    nh = A_HEADS
    return pl.pallas_call(
        _attn_a_kernel,
        out_shape=jax.ShapeDtypeStruct((b, s, A_WIDTH), BF16),
        grid=(b, nh),
        in_specs=[
            pl.BlockSpec(memory_space=pltpu.SMEM),
            pl.BlockSpec((1, s, A_HEAD_DIM), lambda bi, h: (bi, 0, h)),
            pl.BlockSpec((1, s, A_HEAD_DIM), lambda bi, h: (bi, 0, nh + h)),
            pl.BlockSpec((1, s, A_HEAD_DIM), lambda bi, h: (bi, 0, 2 * nh + h)),
        ],
        out_specs=pl.BlockSpec((1, s, A_HEAD_DIM), lambda bi, h: (bi, 0, h)),
        scratch_shapes=[pltpu.VMEM((len(A_BRANCHES), s, LANES), F32)] * 3,
        compiler_params=_cparams("arbitrary", "arbitrary"),
        name="attn_a",
    )(slopes, proj3, proj3, proj3)


PAIRS_PER_KV = B_GROUP * B_HEAD_DIM // LANES


def _attn_b_kernel(slopes_ref, sinks_ref, q_ref, k_ref, v_ref, o_ref, k2_ref, v2_ref):
    seq = q_ref.shape[1]
    n_blk = seq // BLK
    pair = pl.program_id(1)
    kv_is0 = (pair // PAIRS_PER_KV) == 0
    scale = B_HEAD_DIM ** -0.5
    max_delta = B_WINDOW - 1
    lane = lax.broadcasted_iota(jnp.int32, (BLK, LANES), 1)
    lo_half = lane < B_HEAD_DIM

    row2 = lax.broadcasted_iota(jnp.int32, (2 * BLK, LANES), 0)
    lane2 = lax.broadcasted_iota(jnp.int32, (2 * BLK, LANES), 1)
    head_ones = jnp.where((row2 < BLK) == (lane2 < B_HEAD_DIM), 1.0, 0.0).astype(BF16)

    @pl.when(pair % PAIRS_PER_KV == 0)
    def _():
        def stage(bi, carry):
            r = pl.ds(pl.multiple_of(bi * BLK, BLK), BLK)
            r2 = pl.ds(pl.multiple_of(bi * 2 * BLK, 2 * BLK), 2 * BLK)
            staged = []
            for src in (k_ref, v_ref):
                t = src[0, r, :]
                t_rot = pltpu.roll(t, B_HEAD_DIM, axis=1)
                in_lo = jnp.where(kv_is0, t, t_rot)
                in_hi = jnp.where(kv_is0, t_rot, t)
                staged.append(jnp.concatenate([jnp.where(lo_half, in_lo, 0.0),
                                               jnp.where(lo_half, 0.0, in_hi)], axis=0).astype(BF16))
            k2_ref[r2, :] = staged[0]
            v2_ref[r2, :] = jnp.concatenate([staged[1], head_ones], axis=1)
            return carry
        lax.fori_loop(0, n_blk, stage, 0)

    qi = lax.broadcasted_iota(jnp.int32, (BLK, BLK), 0)
    kj = lax.broadcasted_iota(jnp.int32, (BLK, BLK), 1)
    d_cur = qi - kj
    d_prev = d_cur + BLK
    assert max_delta == BLK - 1
    in_prev = kj > qi
    heads = (2 * pair, 2 * pair + 1)
    bias_fold, bias_first, sink = [], [], []
    for hq in heads:
        slope = slopes_ref[hq]
        b_cur = -(slope * d_cur.astype(F32))
        bias_fold.append(jnp.where(in_prev, -(slope * d_prev.astype(F32)), b_cur))
        bias_first.append(jnp.where(in_prev, NEG_INF, b_cur))
        sink.append(sinks_ref[hq])

    def staged_rows(bi):
        first = max(bi - 1, 0)
        return pl.ds(first * 2 * BLK, (bi + 1 - first) * 2 * BLK)

    def scores(bi, _):
        q = (q_ref[0, pl.ds(bi * BLK, BLK), :] * scale).astype(BF16)
        return _nt_dot(q, k2_ref[staged_rows(bi), :])

    def softmax(bi, s_all):
        has_prev = bi > 0
        p_c, p_p, sink_term = [], [], []
        for j in range(2):
            if has_prev:
                s_p = s_all[:, j * BLK:(j + 1) * BLK]
                s_c = s_all[:, (2 + j) * BLK:(3 + j) * BLK]
                s = jnp.where(in_prev, s_p, s_c) + bias_fold[j]
            else:
                s = s_all[:, j * BLK:(j + 1) * BLK] + bias_first[j]
            m = jnp.maximum(jnp.max(s, axis=-1, keepdims=True), sink[j])
            p = jnp.exp(s - m)
            sink_term.append(jnp.exp(sink[j] - m))
            if has_prev:
                pp = jnp.where(in_prev, p, 0.0)
                p_p.append(pp.astype(BF16))
                p_c.append((p - pp).astype(BF16))
            else:
                p_c.append(p.astype(BF16))
        return jnp.concatenate(p_p + p_c, axis=1), jnp.where(lo_half, sink_term[0], sink_term[1])

    def values(bi, p_sink):
        p, sink_den = p_sink
        accden = jnp.dot(p, v2_ref[staged_rows(bi), :], preferred_element_type=F32)
        o_ref[0, pl.ds(bi * BLK, BLK), :] = (accden[:, :LANES] / (accden[:, LANES:] + sink_den)).astype(o_ref.dtype)

    _software_pipeline(list(range(n_blk)), (scores, softmax, values))


def _attn_b(proj3, slopes, sinks):
    b, s, _ = proj3.shape
    q0 = 3 * A_WIDTH // LANES
    kblk = (3 * A_WIDTH + B_WIDTH) // LANES
    assert B_KV_HEADS * B_HEAD_DIM == LANES
    return pl.pallas_call(
        _attn_b_kernel,
        out_shape=jax.ShapeDtypeStruct((b, s, B_WIDTH), BF16),
        grid=(b, B_WIDTH // LANES),
        in_specs=[
            pl.BlockSpec(memory_space=pltpu.SMEM),
            pl.BlockSpec(memory_space=pltpu.SMEM),
            pl.BlockSpec((1, s, LANES), lambda bi, p: (bi, 0, q0 + p)),
            pl.BlockSpec((1, s, LANES), lambda bi, p: (bi, 0, kblk)),
            pl.BlockSpec((1, s, LANES), lambda bi, p: (bi, 0, kblk + 1)),
        ],
        out_specs=pl.BlockSpec((1, s, LANES), lambda bi, p: (bi, 0, p)),
        scratch_shapes=[
            pltpu.VMEM((2 * s, LANES), BF16),
            pltpu.VMEM((2 * s, 2 * LANES), BF16),
        ],
        compiler_params=_cparams("arbitrary", "arbitrary"),
        name="attn_b",
    )(slopes, sinks, proj3, proj3, proj3)


GRP_LANE0 = N_EXPERTS
ROUTE_FIELDS = 6


def _outproj_kernel(x_ref, ma_ref, mb_ref, wo_ref, g_ref, wr_ref, br_ref,
                    h1_ref, m_ref, route_ref, route_t_ref, cnt_ref, carry_ref):
    tm = x_ref.shape[0]

    @pl.when(pl.program_id(0) == 0)
    def _():
        carry_ref[...] = jnp.zeros_like(carry_ref)

    h1 = (x_ref[...]
          + jnp.dot(ma_ref[...], wo_ref[0:A_WIDTH, :], preferred_element_type=F32)
          + jnp.dot(mb_ref[...], wo_ref[A_WIDTH:, :], preferred_element_type=F32))
    h1_ref[...] = h1
    m = _rms(h1, g_ref[...])
    m_ref[...] = m.reshape(m_ref.shape)
    logits = jnp.dot(m.astype(BF16), wr_ref[...], preferred_element_type=F32) + br_ref[...]

    lane = lax.broadcasted_iota(jnp.int32, (tm, LANES), 1).astype(F32)
    big = float(LANES)
    gl = jnp.where((lane >= GRP_LANE0) & (lane < GRP_LANE0 + N_GROUPS), logits, NEG_INF)
    gmax = jnp.max(gl, axis=-1, keepdims=True)
    gidx = jnp.min(jnp.where(gl == gmax, lane, big), axis=-1, keepdims=True) - GRP_LANE0
    grp_w = 1.0 / jnp.sum(jnp.exp(gl - gmax), axis=-1, keepdims=True)
    e_lo = gidx * EXPERTS_PER_GROUP
    el = jnp.where((lane >= e_lo) & (lane < e_lo + EXPERTS_PER_GROUP), logits, NEG_INF)
    v1 = jnp.max(el, axis=-1, keepdims=True)
    i1 = jnp.min(jnp.where(el == v1, lane, big), axis=-1, keepdims=True)
    el2 = jnp.where(lane == i1, NEG_INF, el)
    v2 = jnp.max(el2, axis=-1, keepdims=True)
    i2 = jnp.min(jnp.where(el2 == v2, lane, big), axis=-1, keepdims=True)
    e2 = jnp.exp(v2 - v1)
    w1 = (1.0 / (1.0 + e2)) * grp_w
    w2 = (e2 / (1.0 + e2)) * grp_w

    onehot = jnp.where(lane == i1, 1.0, 0.0) + jnp.where(lane == i2, 1.0, 0.0)
    row = lax.broadcasted_iota(jnp.int32, (tm, tm), 0)
    col = lax.broadcasted_iota(jnp.int32, (tm, tm), 1)
    earlier = jnp.where(row > col, 1.0, 0.0).astype(BF16)
    before = carry_ref[0:1, :] + jnp.dot(earlier, onehot.astype(BF16), preferred_element_type=F32)
    r1 = jnp.sum(jnp.where(lane == i1, before, 0.0), axis=-1, keepdims=True)
    r2 = jnp.sum(jnp.where(lane == i2, before, 0.0), axis=-1, keepdims=True)
    carry_ref[...] = carry_ref[...] + jnp.sum(onehot, axis=0, keepdims=True)
    cnt_ref[...] = carry_ref[...]

    route = jnp.zeros((tm, LANES), F32)
    for idx, val in enumerate((i1, i2, r1, r2, w1, w2)):
        route = jnp.where(lane == float(idx), val, route)
    route_ref[...] = route
    route_t_ref[...] = jnp.transpose(route)[0:SUBLANES, :]


def _outproj(x2, ma, mb, wo_bf16, g, wr_bf16, br):
    t, d = x2.shape
    tm = ROW_TILE
    row_spec = lambda w: pl.BlockSpec((tm, w), lambda i: (i, 0))
    const = lambda shape: pl.BlockSpec(shape, lambda i: (0, 0))
    return pl.pallas_call(
        _outproj_kernel,
        out_shape=(
            jax.ShapeDtypeStruct((t, d), F32),
            jax.ShapeDtypeStruct((t // SUBLANES, SUBLANES, d), F32),
            jax.ShapeDtypeStruct((t, LANES), F32),
            jax.ShapeDtypeStruct((SUBLANES, t), F32),
            jax.ShapeDtypeStruct((8, LANES), F32),
        ),
        grid=(t // tm,),
        in_specs=[
            row_spec(d), row_spec(A_WIDTH), row_spec(B_WIDTH),
            pl.BlockSpec((d, d), lambda i: (0, 0), pipeline_mode=pl.Buffered(1)),
            const((1, d)), const((d, LANES)), const((1, LANES)),
        ],
        out_specs=(row_spec(d),
                   pl.BlockSpec((tm // SUBLANES, SUBLANES, d), lambda i: (i, 0, 0)),
                   row_spec(LANES), pl.BlockSpec((SUBLANES, tm), lambda i: (0, i)), const((8, LANES))),
        scratch_shapes=[pltpu.VMEM((8, LANES), F32)],
        compiler_params=_cparams("arbitrary"),
        name="outproj",
    )(x2, ma, mb, wo_bf16, g, wr_bf16, br)


def _dispatch_kernel(pos_ref, m_ref, xs_ref, sem):
    groups = m_ref.shape[0]
    base = pl.program_id(0) * (groups * SUBLANES * TOP_K)

    def start(g, c):
        for j in range(SUBLANES):
            for k in range(TOP_K):
                dst = pos_ref[base + TOP_K * (g * SUBLANES + j) + k]
                pltpu.make_async_copy(m_ref.at[g, pl.ds(j, 1)], xs_ref.at[pl.ds(dst, 1)], sem).start()
        return c
    lax.fori_loop(0, groups, start, 0, unroll=DMA_UNROLL // SUBLANES)

    def wait(g, c):
        for _ in range(SUBLANES * TOP_K):
            pltpu.make_async_copy(m_ref.at[0, pl.ds(0, 1)], xs_ref.at[pl.ds(0, 1)], sem).wait()
        return c
    lax.fori_loop(0, groups, wait, 0, unroll=DMA_UNROLL // SUBLANES)


def _dispatch(pos_flat, m3):
    groups, _, w = m3.shape
    t = groups * SUBLANES
    tm = ROW_TILE
    return pl.pallas_call(
        _dispatch_kernel,
        out_shape=jax.ShapeDtypeStruct((t * TOP_K, w), m3.dtype),
        grid_spec=pltpu.PrefetchScalarGridSpec(
            num_scalar_prefetch=1,
            grid=(t // tm,),
            in_specs=[pl.BlockSpec((tm // SUBLANES, SUBLANES, w), lambda i, pos: (i, 0, 0))],
            out_specs=pl.BlockSpec(memory_space=pl.ANY),
            scratch_shapes=[pltpu.SemaphoreType.DMA],
        ),
        compiler_params=_cparams("arbitrary"),
        name="dispatch",
    )(pos_flat, m3)


def _experts_kernel(tile_ref, exp_ref, lo_ref, hi_ref, first_ref, newexp_ref,
                    x_ref, wg_ref, wu_ref, wd_ref, y_ref, wg_bf, wu_bf, wd_bf):
    i = pl.program_id(0)
    lo, hi = lo_ref[i], hi_ref[i]

    @pl.when(newexp_ref[i] == 1)
    def _():
        wg_bf[...] = wg_ref[0].astype(BF16)
        wu_bf[...] = wu_ref[0].astype(BF16)
        wd_bf[...] = wd_ref[0].astype(BF16)

    @pl.when(hi > lo)
    def _():
        x = x_ref[...].astype(BF16)
        hg = jnp.dot(x, wg_bf[...], preferred_element_type=F32)
        hu = jnp.dot(x, wu_bf[...], preferred_element_type=F32)
        hid = (hg * jax.nn.sigmoid(hg)) * hu
        y = jnp.dot(hid.astype(BF16), wd_bf[...], preferred_element_type=F32)
        row = lax.broadcasted_iota(jnp.int32, (x_ref.shape[0], 1), 0)
        mine = (row >= lo) & (row < hi)

        @pl.when(first_ref[i] == 1)
        def _():
            y_ref[...] = jnp.where(mine, y, 0.0)

        @pl.when(first_ref[i] == 0)
        def _():
            y_ref[...] = jnp.where(mine, y, y_ref[...])


def _experts(meta, xs, wg, wu, wd):
    n, w = xs.shape
    tm = EXPERT_TILE
    n_items = meta[0].shape[0]
    d, f = wg.shape[-2:]
    assert d == w
    idx = lambda fn: (lambda i, tl, ex, lo, hi, fi, ne: fn(i, tl, ex))
    return pl.pallas_call(
        _experts_kernel,
        out_shape=jax.ShapeDtypeStruct((n, w), xs.dtype),
        grid_spec=pltpu.PrefetchScalarGridSpec(
            num_scalar_prefetch=len(meta),
            grid=(n_items,),
            in_specs=[
                pl.BlockSpec((tm, w), idx(lambda i, tl, ex: (tl[i], 0))),
                pl.BlockSpec((1, d, f), idx(lambda i, tl, ex: (ex[i], 0, 0))),
                pl.BlockSpec((1, d, f), idx(lambda i, tl, ex: (ex[i], 0, 0))),
                pl.BlockSpec((1, f, d), idx(lambda i, tl, ex: (ex[i], 0, 0))),
            ],
            out_specs=pl.BlockSpec((tm, w), idx(lambda i, tl, ex: (tl[i], 0))),
            scratch_shapes=[pltpu.VMEM((d, f), BF16), pltpu.VMEM((d, f), BF16), pltpu.VMEM((f, d), BF16)],
        ),
        compiler_params=_cparams("arbitrary"),
        name="experts",
    )(*meta, xs, wg, wu, wd)


def _expert_work_items(counts, n_rows):
    tm = EXPERT_TILE
    n_tiles = n_rows // tm
    n_items = n_tiles + N_EXPERTS - 1
    offs = jnp.cumsum(counts) - counts
    ends = offs + counts
    t_first = offs // tm
    t_last = jnp.where(counts > 0, (ends - 1) // tm, t_first - 1)
    per_e = t_last - t_first + 1
    item_end = jnp.cumsum(per_e)
    item_start = item_end - per_e
    total = item_end[-1]
    ids = jnp.arange(n_items, dtype=jnp.int32)
    e = jnp.minimum(jnp.sum(item_end[None, :] <= ids[:, None], axis=1), N_EXPERTS - 1).astype(jnp.int32)
    tile = t_first[e] + (ids - item_start[e])
    lo = jnp.clip(offs[e] - tile * tm, 0, tm)
    hi = jnp.clip(ends[e] - tile * tm, 0, tm)
    valid = ids < total
    last = jnp.maximum(total - 1, 0)
    tile = jnp.where(valid, tile, tile[last])
    e = jnp.where(valid, e, e[last])
    lo = jnp.where(valid, lo, 0)
    hi = jnp.where(valid, hi, 0)
    prev_tile = jnp.concatenate([jnp.full((1,), -1, tile.dtype), tile[:-1]])
    first = (valid & (tile != prev_tile)).astype(jnp.int32)
    prev_e = jnp.concatenate([jnp.full((1,), -1, e.dtype), e[:-1]])
    new_expert = e != prev_e
    return tuple(a.astype(jnp.int32) for a in (tile, e, lo, hi, first, new_expert))


def _final_kernel(pos_ref, h1_ref, route_ref, p_ref, wple_ref, wpg_ref, gple_ref, gfin_ref, ys_ref,
                  o_ref, ybuf_ref, h_ref, sem):
    tm = h1_ref.shape[0]
    groups = tm // SUBLANES
    i = pl.program_id(0)
    n = pl.num_programs(0)
    slot = i % 2

    def start_group(tile, slot_, g):
        base = tile * (tm * TOP_K)
        for j in range(SUBLANES):
            for k in range(TOP_K):
                src = pos_ref[base + TOP_K * (g * SUBLANES + j) + k]
                pltpu.make_async_copy(ys_ref.at[pl.ds(src, 1)], ybuf_ref.at[slot_, k, g, pl.ds(j, 1)],
                                      sem.at[slot_]).start()

    def wait_tile(slot_):
        for _ in range(tm * TOP_K):
            pltpu.make_async_copy(ys_ref.at[pl.ds(0, 1)], ybuf_ref.at[slot_, 0, 0, pl.ds(0, 1)],
                                  sem.at[slot_]).wait()

    @pl.when(i == 0)
    def _():
        def body(g, c):
            start_group(0, 0, g)
            return c
        lax.fori_loop(0, groups, body, 0, unroll=DMA_UNROLL // SUBLANES)

    wait_tile(slot)

    lane = lax.broadcasted_iota(jnp.int32, (tm, LANES), 1)
    route = route_ref[...]
    w0 = jnp.sum(jnp.where(lane == 4, route, 0.0), axis=-1, keepdims=True)
    w1 = jnp.sum(jnp.where(lane == 5, route, 0.0), axis=-1, keepdims=True)
    d = ybuf_ref.shape[-1]
    moe = w0 * ybuf_ref[slot, 0].reshape(tm, d) + w1 * ybuf_ref[slot, 1].reshape(tm, d)
    h_ref[...] = h1_ref[...] + moe

    nxt = jnp.minimum(i + 1, n - 1)
    for g in range(groups):
        start_group(nxt, 1 - slot, g)

    h2 = h_ref[...]
    nrm = _rms(h2, gple_ref[...]).astype(BF16)
    gate = jax.nn.sigmoid(jnp.dot(nrm, wpg_ref[...], preferred_element_type=F32))
    pw = jnp.dot(p_ref[...].astype(BF16), wple_ref[...], preferred_element_type=F32)
    h3 = h2 + gate * pw
    o_ref[...] = _rms(h3, gfin_ref[...])

    @pl.when(i == n - 1)
    def _():
        wait_tile(1 - slot)


def _final(pos_flat, h1, route, p2, wple_bf16, wpg_bf16, gple, gfin, ys):
    t, d = h1.shape
    tm = ROW_TILE
    row_spec = lambda w: pl.BlockSpec((tm, w), lambda i, pos: (i, 0))
    const = lambda shape: pl.BlockSpec(shape, lambda i, pos: (0, 0))
    return pl.pallas_call(
        _final_kernel,
        out_shape=jax.ShapeDtypeStruct((t, d), F32),
        grid_spec=pltpu.PrefetchScalarGridSpec(
            num_scalar_prefetch=1,
            grid=(t // tm,),
            in_specs=[
                row_spec(d), row_spec(LANES), row_spec(PLE_DIM),
                const((PLE_DIM, d)),
                pl.BlockSpec((d, d), lambda i, pos: (0, 0), pipeline_mode=pl.Buffered(1)),
                const((1, d)), const((1, d)),
                pl.BlockSpec(memory_space=pl.ANY),
            ],
            out_specs=row_spec(d),
            scratch_shapes=[
                pltpu.VMEM((2, TOP_K, tm // SUBLANES, SUBLANES, ys.shape[1]), ys.dtype),
                pltpu.VMEM((tm, d), F32),
                pltpu.SemaphoreType.DMA((2,)),
            ],
        ),
        compiler_params=_cparams("arbitrary"),
        name="final",
    )(pos_flat, h1, route, p2, wple_bf16, wpg_bf16, gple, gfin, ys)


def _router_weights(w_grp, b_grp, w_exp, b_exp):
    d = w_grp.shape[0]
    pad = LANES - N_EXPERTS - N_GROUPS
    wr = jnp.concatenate([w_exp.reshape(d, N_EXPERTS), w_grp, jnp.zeros((d, pad), F32)], axis=1)
    br = jnp.concatenate([b_exp.reshape(N_EXPERTS), b_grp, jnp.zeros((pad,), F32)]).reshape(1, LANES)
    return wr.astype(BF16), br.astype(F32)


def kernel(x, p, w_in, w_out, sinks, g_mix, g_moe, g_ple, g_final, w_grp, b_grp, w_exp, b_exp,
           w_gate, w_up, w_down, w_ple, w_ple_gate):
    b, s, d = x.shape
    t = b * s
    depth = w_in.shape[0]
    assert d == D_MODEL and t % ROW_TILE == 0 and (t * TOP_K) % EXPERT_TILE == 0
    assert depth == 1, "the final rmsnorm is fused into the (single) layer's last kernel"
    slopes_a = _alibi_slopes(A_HEADS)
    slopes_b = _alibi_slopes(B_HEADS)

    h = x.reshape(t, d)
    for i in range(depth):
        proj = _proj(h, g_mix[i].reshape(1, d), w_in[i].astype(BF16))
        proj3 = proj.reshape(b, s, IN_COLS)
        mixed_a = _attn_a(proj3, slopes_a).reshape(t, A_WIDTH)
        mixed_b = _attn_b(proj3, slopes_b, sinks[i]).reshape(t, B_WIDTH)

        wr, br = _router_weights(w_grp[i], b_grp[i], w_exp[i], b_exp[i])
        h1, m, route, route_t, cnt = _outproj(h, mixed_a, mixed_b, w_out[i].astype(BF16),
                                              g_moe[i].reshape(1, d), wr, br)

        counts = cnt[0, :N_EXPERTS].astype(jnp.int32)
        offs = jnp.cumsum(counts) - counts
        eid = route_t[0:TOP_K].astype(jnp.int32)
        rank = route_t[TOP_K:2 * TOP_K].astype(jnp.int32)
        experts = jnp.arange(N_EXPERTS, dtype=jnp.int32)[:, None, None]
        seg_start = jnp.sum(jnp.where(eid[None] == experts, offs[:, None, None], 0), axis=0)
        pos_flat = (seg_start + rank).T.reshape(t * TOP_K)

        xs = _dispatch(pos_flat, m)
        ys = _experts(_expert_work_items(counts, t * TOP_K), xs, w_gate[i], w_up[i], w_down[i])
        h_next = _final(pos_flat, h1, route, p[i].reshape(t, PLE_DIM), w_ple[i].astype(BF16),
                        w_ple_gate[i].astype(BF16), g_ple[i].reshape(1, d), g_final.reshape(1, d), ys)
        h = h_next
    return h.reshape(b, s, d)
```

```python
import functools

import numpy as np
import jax
import jax.numpy as jnp
from jax import lax
from jax.experimental import pallas as pl
from jax.experimental.pallas import tpu as pltpu

D_MODEL = 2048
PLE_DIM = 256
BLK = 128
EPS = 1e-6
A_HEAD_DIM = 128
A_WIDTH = D_MODEL // 2
A_HEADS = A_WIDTH // A_HEAD_DIM
A_BRANCHES = ((128, 1), (512, 4), (2048, 16))
B_HEAD_DIM = 64
B_WIDTH = D_MODEL - A_WIDTH
B_HEADS = B_WIDTH // B_HEAD_DIM
B_GROUP = 8
B_KV_HEADS = B_HEADS // B_GROUP
B_WINDOW = 128
IN_COLS = 3 * A_WIDTH + B_WIDTH + 2 * B_KV_HEADS * B_HEAD_DIM
N_GROUPS = 4
EXPERTS_PER_GROUP = 8
N_EXPERTS = N_GROUPS * EXPERTS_PER_GROUP
TOP_K = 2
D_EXPERT = D_MODEL // 8

LANES = 128
SUBLANES = 8
V7X_VMEM_LIMIT_BYTES = 56 * 1024 * 1024

ROW_TILE = 512
EXPERT_TILE = 256
DMA_UNROLL = 16
F32 = jnp.float32
BF16 = jnp.bfloat16
NEG_INF = float("-inf")
LOG2E = 1.4426950408889634


def _cparams(*sem):
    return pltpu.CompilerParams(dimension_semantics=sem, vmem_limit_bytes=V7X_VMEM_LIMIT_BYTES)


def _rms(x, g):
    return x * lax.rsqrt(jnp.mean(x * x, axis=-1, keepdims=True) + EPS) * g


def _alibi_slopes(n):
    return jnp.asarray(np.array([2.0 ** (-8.0 * (i + 1) / n) for i in range(n)], dtype=np.float32))


def _proj_kernel(x_ref, g_ref, w_ref, o_ref):
    a = _rms(x_ref[...], g_ref[...]).astype(BF16)
    o_ref[...] = jnp.dot(a, w_ref[...], preferred_element_type=F32)


def _proj(x2, g, w_bf16):
    t, d = x2.shape
    n = w_bf16.shape[1]
    return pl.pallas_call(
        _proj_kernel,
        out_shape=jax.ShapeDtypeStruct((t, n), F32),
        grid=(t // ROW_TILE,),
        in_specs=[
            pl.BlockSpec((ROW_TILE, d), lambda i: (i, 0)),
            pl.BlockSpec((1, d), lambda i: (0, 0)),
            pl.BlockSpec((d, n), lambda i: (0, 0), pipeline_mode=pl.Buffered(1)),
        ],
        out_specs=pl.BlockSpec((ROW_TILE, n), lambda i: (i, 0)),
        compiler_params=_cparams("arbitrary"),
        name="proj",
    )(x2, g, w_bf16)


def _software_pipeline(items, stages):
    state = [None] * len(items)
    for t in range(len(items) + len(stages) - 1):
        for s, stage in enumerate(stages):
            j = t - s
            if 0 <= j < len(items):
                state[j] = stage(items[j], state[j])


def _nt_dot(a, b):
    return lax.dot_general(a, b, (((1,), (1,)), ((), ())), preferred_element_type=F32)


def _attn_a_kernel(slopes_ref, q_ref, k_ref, v_ref, o_ref, acc_ref, m_ref, den_ref):
    seq = q_ref.shape[1]
    slope = slopes_ref[pl.program_id(1)]
    scale2 = A_HEAD_DIM ** -0.5 * LOG2E
    qi = lax.broadcasted_iota(jnp.int32, (BLK, BLK), 0)
    kj = lax.broadcasted_iota(jnp.int32, (BLK, BLK), 1)
    d_cur = qi - kj
    d_prev = d_cur + BLK

    items = []
    for br, (window, dil) in enumerate(A_BRANCHES):
        max_delta = window // dil
        assert max_delta <= BLK and seq % (dil * BLK) == 0
        bias_cur = jnp.where((d_cur >= 0) & (d_cur <= max_delta),
                             -(slope * (dil * d_cur).astype(F32)) * LOG2E, NEG_INF)
        bias_prev = jnp.where((d_prev >= 0) & (d_prev <= max_delta),
                              -(slope * (dil * d_prev).astype(F32)) * LOG2E, NEG_INF)
        bias_both = jnp.concatenate([bias_prev, bias_cur], axis=1)
        for bi in range(seq // (dil * BLK)):
            for r in range(dil):
                items.append(dict(br=br, dil=dil, start=bi * dil * BLK + r, has_prev=bi > 0,
                                  bias=bias_both if bi > 0 else bias_cur))

    def rows(it, start):
        return pl.ds(start, BLK, stride=it["dil"]) if it["dil"] > 1 else pl.ds(start, BLK)

    def keys_or_values(ref, it):
        cur = ref[0, rows(it, it["start"]), :].astype(BF16)
        if not it["has_prev"]:
            return cur
        prev = ref[0, rows(it, it["start"] - it["dil"] * BLK), :].astype(BF16)
        return jnp.concatenate([prev, cur], axis=0)

    def scores(it, _):
        q = q_ref[0, rows(it, it["start"]), :].astype(BF16)
        return _nt_dot(q, keys_or_values(k_ref, it)) * scale2 + it["bias"]

    def softmax(it, s):
        m = jnp.max(s, axis=-1, keepdims=True)
        return m, jnp.exp2(s - m).astype(BF16)

    def values(it, m_p):
        m, p = m_p
        v = keys_or_values(v_ref, it)
        accden = jnp.dot(p, jnp.concatenate([v, jnp.ones_like(v)], axis=1), preferred_element_type=F32)
        r = rows(it, it["start"])
        acc_ref[it["br"], r, :] = accden[:, :LANES]
        den_ref[it["br"], r, :] = accden[:, LANES:]
        m_ref[it["br"], r, :] = jnp.broadcast_to(m, (BLK, LANES))

    _software_pipeline(items, (scores, softmax, values))

    chunk = 256
    def mix(ci, carry):
        r = pl.ds(pl.multiple_of(ci * chunk, chunk), chunk)
        m0, m1, m2 = m_ref[0, r, :], m_ref[1, r, :], m_ref[2, r, :]
        mx = jnp.maximum(jnp.maximum(m0, m1), m2)
        a0, a1, a2 = jnp.exp2(m0 - mx), jnp.exp2(m1 - mx), jnp.exp2(m2 - mx)
        num = a0 * acc_ref[0, r, :] + a1 * acc_ref[1, r, :] + a2 * acc_ref[2, r, :]
        tot = a0 * den_ref[0, r, :] + a1 * den_ref[1, r, :] + a2 * den_ref[2, r, :]
        o_ref[0, r, :] = (num / tot).astype(o_ref.dtype)
        return carry
    lax.fori_loop(0, seq // chunk, mix, 0)


def _attn_a(proj3, slopes):
    b, s, _ = proj3.shape
    nh = A_HEADS
    return pl.pallas_call(
        _attn_a_kernel,
        out_shape=jax.ShapeDtypeStruct((b, s, A_WIDTH), BF16),
        grid=(b, nh),
        in_specs=[
            pl.BlockSpec(memory_space=pltpu.SMEM),
            pl.BlockSpec((1, s, A_HEAD_DIM), lambda bi, h: (bi, 0, h)),
            pl.BlockSpec((1, s, A_HEAD_DIM), lambda bi, h: (bi, 0, nh + h)),
            pl.BlockSpec((1, s, A_HEAD_DIM), lambda bi, h: (bi, 0, 2 * nh + h)),
        ],
        out_specs=pl.BlockSpec((1, s, A_HEAD_DIM), lambda bi, h: (bi, 0, h)),
        scratch_shapes=[pltpu.VMEM((len(A_BRANCHES), s, LANES), F32)] * 3,
        compiler_params=_cparams("arbitrary", "arbitrary"),
        name="attn_a",
    )(slopes, proj3, proj3, proj3)


PAIRS_PER_KV = B_GROUP * B_HEAD_DIM // LANES


def _attn_b_kernel(slopes_ref, sinks_ref, q_ref, k_ref, v_ref, o_ref, k2_ref, v2_ref):
    seq = q_ref.shape[1]
    n_blk = seq // BLK
    pair = pl.program_id(1)
    kv_is0 = (pair // PAIRS_PER_KV) == 0
    scale = B_HEAD_DIM ** -0.5
    max_delta = B_WINDOW - 1
    lane = lax.broadcasted_iota(jnp.int32, (BLK, LANES), 1)
    lo_half = lane < B_HEAD_DIM

    row2 = lax.broadcasted_iota(jnp.int32, (2 * BLK, LANES), 0)
    lane2 = lax.broadcasted_iota(jnp.int32, (2 * BLK, LANES), 1)
    head_ones = jnp.where((row2 < BLK) == (lane2 < B_HEAD_DIM), 1.0, 0.0).astype(BF16)

    @pl.when(pair % PAIRS_PER_KV == 0)
    def _():
        def stage(bi, carry):
            r = pl.ds(pl.multiple_of(bi * BLK, BLK), BLK)
            r2 = pl.ds(pl.multiple_of(bi * 2 * BLK, 2 * BLK), 2 * BLK)
            staged = []
            for src in (k_ref, v_ref):
                t = src[0, r, :]
                t_rot = pltpu.roll(t, B_HEAD_DIM, axis=1)
                in_lo = jnp.where(kv_is0, t, t_rot)
                in_hi = jnp.where(kv_is0, t_rot, t)
                staged.append(jnp.concatenate([jnp.where(lo_half, in_lo, 0.0),
                                               jnp.where(lo_half, 0.0, in_hi)], axis=0).astype(BF16))
            k2_ref[r2, :] = staged[0]
            v2_ref[r2, :] = jnp.concatenate([staged[1], head_ones], axis=1)
            return carry
        lax.fori_loop(0, n_blk, stage, 0)

    qi = lax.broadcasted_iota(jnp.int32, (BLK, BLK), 0)
    kj = lax.broadcasted_iota(jnp.int32, (BLK, BLK), 1)
    d_cur = qi - kj
    d_prev = d_cur + BLK
    assert max_delta == BLK - 1
    in_prev = kj > qi
    heads = (2 * pair, 2 * pair + 1)
    bias_fold, bias_first, sink = [], [], []
    for hq in heads:
        slope = slopes_ref[hq]
        b_cur = -(slope * d_cur.astype(F32))
        bias_fold.append(jnp.where(in_prev, -(slope * d_prev.astype(F32)), b_cur))
        bias_first.append(jnp.where(in_prev, NEG_INF, b_cur))
        sink.append(sinks_ref[hq])

    def staged_rows(bi):
        first = max(bi - 1, 0)
        return pl.ds(first * 2 * BLK, (bi + 1 - first) * 2 * BLK)

    def scores(bi, _):
        q = (q_ref[0, pl.ds(bi * BLK, BLK), :] * scale).astype(BF16)
        return _nt_dot(q, k2_ref[staged_rows(bi), :])

    def softmax(bi, s_all):
        has_prev = bi > 0
        p_c, p_p, sink_term = [], [], []
        for j in range(2):
            if has_prev:
                s_p = s_all[:, j * BLK:(j + 1) * BLK]
                s_c = s_all[:, (2 + j) * BLK:(3 + j) * BLK]
                s = jnp.where(in_prev, s_p, s_c) + bias_fold[j]
            else:
                s = s_all[:, j * BLK:(j + 1) * BLK] + bias_first[j]
            m = jnp.maximum(jnp.max(s, axis=-1, keepdims=True), sink[j])
            p = jnp.exp(s - m)
            sink_term.append(jnp.exp(sink[j] - m))
            if has_prev:
                pp = jnp.where(in_prev, p, 0.0)
                p_p.append(pp.astype(BF16))
                p_c.append((p - pp).astype(BF16))
            else:
                p_c.append(p.astype(BF16))
        return jnp.concatenate(p_p + p_c, axis=1), jnp.where(lo_half, sink_term[0], sink_term[1])

    def values(bi, p_sink):
        p, sink_den = p_sink
        accden = jnp.dot(p, v2_ref[staged_rows(bi), :], preferred_element_type=F32)
        o_ref[0, pl.ds(bi * BLK, BLK), :] = (accden[:, :LANES] / (accden[:, LANES:] + sink_den)).astype(o_ref.dtype)

    _software_pipeline(list(range(n_blk)), (scores, softmax, values))


def _attn_b(proj3, slopes, sinks):
    b, s, _ = proj3.shape
    q0 = 3 * A_WIDTH // LANES
    kblk = (3 * A_WIDTH + B_WIDTH) // LANES
    assert B_KV_HEADS * B_HEAD_DIM == LANES
    return pl.pallas_call(
        _attn_b_kernel,
        out_shape=jax.ShapeDtypeStruct((b, s, B_WIDTH), BF16),
        grid=(b, B_WIDTH // LANES),
        in_specs=[
            pl.BlockSpec(memory_space=pltpu.SMEM),
            pl.BlockSpec(memory_space=pltpu.SMEM),
            pl.BlockSpec((1, s, LANES), lambda bi, p: (bi, 0, q0 + p)),
            pl.BlockSpec((1, s, LANES), lambda bi, p: (bi, 0, kblk)),
            pl.BlockSpec((1, s, LANES), lambda bi, p: (bi, 0, kblk + 1)),
        ],
        out_specs=pl.BlockSpec((1, s, LANES), lambda bi, p: (bi, 0, p)),
        scratch_shapes=[
            pltpu.VMEM((2 * s, LANES), BF16),
            pltpu.VMEM((2 * s, 2 * LANES), BF16),
        ],
        compiler_params=_cparams("arbitrary", "arbitrary"),
        name="attn_b",
    )(slopes, sinks, proj3, proj3, proj3)


GRP_LANE0 = N_EXPERTS
ROUTE_FIELDS = 6


def _outproj_kernel(x_ref, ma_ref, mb_ref, wo_ref, g_ref, wr_ref, br_ref,
                    h1_ref, m_ref, route_ref, route_t_ref, cnt_ref, carry_ref):
    tm = x_ref.shape[0]

    @pl.when(pl.program_id(0) == 0)
    def _():
        carry_ref[...] = jnp.zeros_like(carry_ref)

    h1 = (x_ref[...]
          + jnp.dot(ma_ref[...], wo_ref[0:A_WIDTH, :], preferred_element_type=F32)
          + jnp.dot(mb_ref[...], wo_ref[A_WIDTH:, :], preferred_element_type=F32))
    h1_ref[...] = h1
    m = _rms(h1, g_ref[...])
    m_ref[...] = m
    logits = jnp.dot(m.astype(BF16), wr_ref[...], preferred_element_type=F32) + br_ref[...]

    lane = lax.broadcasted_iota(jnp.int32, (tm, LANES), 1).astype(F32)
    big = float(LANES)
    gl = jnp.where((lane >= GRP_LANE0) & (lane < GRP_LANE0 + N_GROUPS), logits, NEG_INF)
    gmax = jnp.max(gl, axis=-1, keepdims=True)
    gidx = jnp.min(jnp.where(gl == gmax, lane, big), axis=-1, keepdims=True) - GRP_LANE0
    grp_w = 1.0 / jnp.sum(jnp.exp(gl - gmax), axis=-1, keepdims=True)
    e_lo = gidx * EXPERTS_PER_GROUP
    el = jnp.where((lane >= e_lo) & (lane < e_lo + EXPERTS_PER_GROUP), logits, NEG_INF)
    v1 = jnp.max(el, axis=-1, keepdims=True)
    i1 = jnp.min(jnp.where(el == v1, lane, big), axis=-1, keepdims=True)
    el2 = jnp.where(lane == i1, NEG_INF, el)
    v2 = jnp.max(el2, axis=-1, keepdims=True)
    i2 = jnp.min(jnp.where(el2 == v2, lane, big), axis=-1, keepdims=True)
    e2 = jnp.exp(v2 - v1)
    w1 = (1.0 / (1.0 + e2)) * grp_w
    w2 = (e2 / (1.0 + e2)) * grp_w

    onehot = jnp.where(lane == i1, 1.0, 0.0) + jnp.where(lane == i2, 1.0, 0.0)
    row = lax.broadcasted_iota(jnp.int32, (tm, tm), 0)
    col = lax.broadcasted_iota(jnp.int32, (tm, tm), 1)
    earlier = jnp.where(row > col, 1.0, 0.0).astype(BF16)
    before = carry_ref[0:1, :] + jnp.dot(earlier, onehot.astype(BF16), preferred_element_type=F32)
    r1 = jnp.sum(jnp.where(lane == i1, before, 0.0), axis=-1, keepdims=True)
    r2 = jnp.sum(jnp.where(lane == i2, before, 0.0), axis=-1, keepdims=True)
    carry_ref[...] = carry_ref[...] + jnp.sum(onehot, axis=0, keepdims=True)
    cnt_ref[...] = carry_ref[...]

    route = jnp.zeros((tm, LANES), F32)
    for idx, val in enumerate((i1, i2, r1, r2, w1, w2)):
        route = jnp.where(lane == float(idx), val, route)
    route_ref[...] = route
    route_t_ref[...] = jnp.transpose(route)[0:SUBLANES, :]


def _outproj(x2, ma, mb, wo_bf16, g, wr_bf16, br):
    t, d = x2.shape
    tm = ROW_TILE
    row_spec = lambda w: pl.BlockSpec((tm, w), lambda i: (i, 0))
    const = lambda shape: pl.BlockSpec(shape, lambda i: (0, 0))
    return pl.pallas_call(
        _outproj_kernel,
        out_shape=(
            jax.ShapeDtypeStruct((t, d), F32),
            jax.ShapeDtypeStruct((t, d), F32),
            jax.ShapeDtypeStruct((t, LANES), F32),
            jax.ShapeDtypeStruct((SUBLANES, t), F32),
            jax.ShapeDtypeStruct((8, LANES), F32),
        ),
        grid=(t // tm,),
        in_specs=[
            row_spec(d), row_spec(A_WIDTH), row_spec(B_WIDTH),
            pl.BlockSpec((d, d), lambda i: (0, 0), pipeline_mode=pl.Buffered(1)),
            const((1, d)), const((d, LANES)), const((1, LANES)),
        ],
        out_specs=(row_spec(d), row_spec(d), row_spec(LANES), pl.BlockSpec((SUBLANES, tm), lambda i: (0, i)), const((8, LANES))),
        scratch_shapes=[pltpu.VMEM((8, LANES), F32)],
        compiler_params=_cparams("arbitrary"),
        name="outproj",
    )(x2, ma, mb, wo_bf16, g, wr_bf16, br)


def _sorted_tokens_kernel(pos_ref, tok_ref):
    def body(t, c):
        for k in range(TOP_K):
            tok_ref[pos_ref[TOP_K * t + k]] = t
        return c
    lax.fori_loop(0, tok_ref.shape[0] // TOP_K, body, 0, unroll=DMA_UNROLL)


def _sorted_tokens(pos_flat):
    return pl.pallas_call(
        _sorted_tokens_kernel,
        out_shape=jax.ShapeDtypeStruct(pos_flat.shape, jnp.int32),
        grid_spec=pltpu.PrefetchScalarGridSpec(
            num_scalar_prefetch=1,
            grid=(1,),
            in_specs=[],
            out_specs=pl.BlockSpec(memory_space=pltpu.SMEM),
        ),
        compiler_params=_cparams("arbitrary"),
        name="sorted_tokens",
    )(pos_flat)


def _experts_kernel(tile_ref, exp_ref, lo_ref, hi_ref, first_ref, newexp_ref, slot_ref, tok_ref,
                    m_ref, wg_ref, wu_ref, wd_ref, y_ref, xbuf, wg_bf, wu_bf, wd_bf, sem):
    i = pl.program_id(0)
    n = pl.num_programs(0)
    tm, d = y_ref.shape
    groups = tm // SUBLANES
    lo, hi = lo_ref[i], hi_ref[i]

    def gather(item):
        base = tile_ref[item] * tm
        s = slot_ref[item]
        def body(g, c):
            for j in range(SUBLANES):
                tok = tok_ref[base + g * SUBLANES + j]
                pltpu.make_async_copy(m_ref.at[pl.ds(tok, 1)], xbuf.at[s, g, pl.ds(j, 1)], sem.at[s]).start()
            return c
        lax.fori_loop(0, groups, body, 0, unroll=DMA_UNROLL // SUBLANES)

    @pl.when(i == 0)
    def _():
        gather(0)

    slot = slot_ref[i]

    @pl.when(first_ref[i] == 1)
    def _():
        for _ in range(tm):
            pltpu.make_async_copy(m_ref.at[pl.ds(0, 1)], xbuf.at[slot, 0, pl.ds(0, 1)], sem.at[slot]).wait()

    nxt = jnp.minimum(i + 1, n - 1)

    @pl.when((i + 1 < n) & (first_ref[nxt] == 1))
    def _():
        gather(nxt)

    @pl.when(newexp_ref[i] == 1)
    def _():
        wg_bf[...] = wg_ref[0].astype(BF16)
        wu_bf[...] = wu_ref[0].astype(BF16)
        wd_bf[...] = wd_ref[0].astype(BF16)

    @pl.when(hi > lo)
    def _():
        x = xbuf[slot].reshape(tm, d).astype(BF16)
        hg = jnp.dot(x, wg_bf[...], preferred_element_type=F32)
        hu = jnp.dot(x, wu_bf[...], preferred_element_type=F32)
        hid = (hg * jax.nn.sigmoid(hg)) * hu
        y = jnp.dot(hid.astype(BF16), wd_bf[...], preferred_element_type=F32)
        row = lax.broadcasted_iota(jnp.int32, (tm, 1), 0)
        mine = (row >= lo) & (row < hi)

        @pl.when(first_ref[i] == 1)
        def _():
            y_ref[...] = jnp.where(mine, y, 0.0)

        @pl.when(first_ref[i] == 0)
        def _():
            y_ref[...] = jnp.where(mine, y, y_ref[...])


def _experts(meta, tok_sorted, m, wg, wu, wd):
    d = m.shape[1]
    n = tok_sorted.shape[0]
    tm = EXPERT_TILE
    n_items = meta[0].shape[0]
    f = wg.shape[-1]
    prefetch = (*meta, tok_sorted)
    idx = lambda fn: (lambda i, tl, ex, *_: fn(i, tl, ex))
    return pl.pallas_call(
        _experts_kernel,
        out_shape=jax.ShapeDtypeStruct((n, d), F32),
        grid_spec=pltpu.PrefetchScalarGridSpec(
            num_scalar_prefetch=len(prefetch),
            grid=(n_items,),
            in_specs=[
                pl.BlockSpec(memory_space=pl.ANY),
                pl.BlockSpec((1, d, f), idx(lambda i, tl, ex: (ex[i], 0, 0))),
                pl.BlockSpec((1, d, f), idx(lambda i, tl, ex: (ex[i], 0, 0))),
                pl.BlockSpec((1, f, d), idx(lambda i, tl, ex: (ex[i], 0, 0))),
            ],
            out_specs=pl.BlockSpec((tm, d), idx(lambda i, tl, ex: (tl[i], 0))),
            scratch_shapes=[
                pltpu.VMEM((2, tm // SUBLANES, SUBLANES, d), F32),
                pltpu.VMEM((d, f), BF16), pltpu.VMEM((d, f), BF16), pltpu.VMEM((f, d), BF16),
                pltpu.SemaphoreType.DMA((2,)),
            ],
        ),
        compiler_params=_cparams("arbitrary"),
        name="experts",
    )(*prefetch, m, wg, wu, wd)


def _expert_work_items(counts, n_rows):
    tm = EXPERT_TILE
    n_tiles = n_rows // tm
    n_items = n_tiles + N_EXPERTS - 1
    offs = jnp.cumsum(counts) - counts
    ends = offs + counts
    t_first = offs // tm
    t_last = jnp.where(counts > 0, (ends - 1) // tm, t_first - 1)
    per_e = t_last - t_first + 1
    item_end = jnp.cumsum(per_e)
    item_start = item_end - per_e
    total = item_end[-1]
    ids = jnp.arange(n_items, dtype=jnp.int32)
    e = jnp.minimum(jnp.sum(item_end[None, :] <= ids[:, None], axis=1), N_EXPERTS - 1).astype(jnp.int32)
    tile = t_first[e] + (ids - item_start[e])
    lo = jnp.clip(offs[e] - tile * tm, 0, tm)
    hi = jnp.clip(ends[e] - tile * tm, 0, tm)
    valid = ids < total
    last = jnp.maximum(total - 1, 0)
    tile = jnp.where(valid, tile, tile[last])
    e = jnp.where(valid, e, e[last])
    lo = jnp.where(valid, lo, 0)
    hi = jnp.where(valid, hi, 0)
    prev_tile = jnp.concatenate([jnp.full((1,), -1, tile.dtype), tile[:-1]])
    first = (valid & (tile != prev_tile)).astype(jnp.int32)
    prev_e = jnp.concatenate([jnp.full((1,), -1, e.dtype), e[:-1]])
    new_expert = e != prev_e
    slot = (jnp.cumsum(first) + 1) % 2
    return tuple(a.astype(jnp.int32) for a in (tile, e, lo, hi, first, new_expert, slot))


def _final_kernel(pos_ref, h1_ref, route_ref, p_ref, wple_ref, wpg_ref, gple_ref, gfin_ref, ys_ref,
                  o_ref, ybuf_ref, h_ref, sem):
    tm = h1_ref.shape[0]
    groups = tm // SUBLANES
    i = pl.program_id(0)
    n = pl.num_programs(0)
    slot = i % 2

    def start_group(tile, slot_, g):
        base = tile * (tm * TOP_K)
        for j in range(SUBLANES):
            for k in range(TOP_K):
                src = pos_ref[base + TOP_K * (g * SUBLANES + j) + k]
                pltpu.make_async_copy(ys_ref.at[pl.ds(src, 1)], ybuf_ref.at[slot_, k, g, pl.ds(j, 1)],
                                      sem.at[slot_]).start()

    def wait_tile(slot_):
        for _ in range(tm * TOP_K):
            pltpu.make_async_copy(ys_ref.at[pl.ds(0, 1)], ybuf_ref.at[slot_, 0, 0, pl.ds(0, 1)],
                                  sem.at[slot_]).wait()

    @pl.when(i == 0)
    def _():
        def body(g, c):
            start_group(0, 0, g)
            return c
        lax.fori_loop(0, groups, body, 0, unroll=DMA_UNROLL // SUBLANES)

    wait_tile(slot)

    lane = lax.broadcasted_iota(jnp.int32, (tm, LANES), 1)
    route = route_ref[...]
    w0 = jnp.sum(jnp.where(lane == 4, route, 0.0), axis=-1, keepdims=True)
    w1 = jnp.sum(jnp.where(lane == 5, route, 0.0), axis=-1, keepdims=True)
    d = ybuf_ref.shape[-1]
    moe = w0 * ybuf_ref[slot, 0].reshape(tm, d) + w1 * ybuf_ref[slot, 1].reshape(tm, d)
    h_ref[...] = h1_ref[...] + moe

    nxt = jnp.minimum(i + 1, n - 1)
    for g in range(groups):
        start_group(nxt, 1 - slot, g)

    h2 = h_ref[...]
    nrm = _rms(h2, gple_ref[...]).astype(BF16)
    gate = jax.nn.sigmoid(jnp.dot(nrm, wpg_ref[...], preferred_element_type=F32))
    pw = jnp.dot(p_ref[...].astype(BF16), wple_ref[...], preferred_element_type=F32)
    h3 = h2 + gate * pw
    o_ref[...] = _rms(h3, gfin_ref[...])

    @pl.when(i == n - 1)
    def _():
        wait_tile(1 - slot)


def _final(pos_flat, h1, route, p2, wple_bf16, wpg_bf16, gple, gfin, ys):
    t, d = h1.shape
    tm = ROW_TILE
    row_spec = lambda w: pl.BlockSpec((tm, w), lambda i, pos: (i, 0))
    const = lambda shape: pl.BlockSpec(shape, lambda i, pos: (0, 0))
    return pl.pallas_call(
        _final_kernel,
        out_shape=jax.ShapeDtypeStruct((t, d), F32),
        grid_spec=pltpu.PrefetchScalarGridSpec(
            num_scalar_prefetch=1,
            grid=(t // tm,),
            in_specs=[
                row_spec(d), row_spec(LANES), row_spec(PLE_DIM),
                const((PLE_DIM, d)),
                pl.BlockSpec((d, d), lambda i, pos: (0, 0), pipeline_mode=pl.Buffered(1)),
                const((1, d)), const((1, d)),
                pl.BlockSpec(memory_space=pl.ANY),
            ],
            out_specs=row_spec(d),
            scratch_shapes=[
                pltpu.VMEM((2, TOP_K, tm // SUBLANES, SUBLANES, ys.shape[1]), ys.dtype),
                pltpu.VMEM((tm, d), F32),
                pltpu.SemaphoreType.DMA((2,)),
            ],
        ),
        compiler_params=_cparams("arbitrary"),
        name="final",
    )(pos_flat, h1, route, p2, wple_bf16, wpg_bf16, gple, gfin, ys)


def _router_weights(w_grp, b_grp, w_exp, b_exp):
    d = w_grp.shape[0]
    pad = LANES - N_EXPERTS - N_GROUPS
    wr = jnp.concatenate([w_exp.reshape(d, N_EXPERTS), w_grp, jnp.zeros((d, pad), F32)], axis=1)
    br = jnp.concatenate([b_exp.reshape(N_EXPERTS), b_grp, jnp.zeros((pad,), F32)]).reshape(1, LANES)
    return wr.astype(BF16), br.astype(F32)


def kernel(x, p, w_in, w_out, sinks, g_mix, g_moe, g_ple, g_final, w_grp, b_grp, w_exp, b_exp,
           w_gate, w_up, w_down, w_ple, w_ple_gate):
    b, s, d = x.shape
    t = b * s
    depth = w_in.shape[0]
    assert d == D_MODEL and t % ROW_TILE == 0 and (t * TOP_K) % EXPERT_TILE == 0
    assert depth == 1, "the final rmsnorm is fused into the (single) layer's last kernel"
    slopes_a = _alibi_slopes(A_HEADS)
    slopes_b = _alibi_slopes(B_HEADS)

    h = x.reshape(t, d)
    for i in range(depth):
        proj = _proj(h, g_mix[i].reshape(1, d), w_in[i].astype(BF16))
        proj3 = proj.reshape(b, s, IN_COLS)
        mixed_a = _attn_a(proj3, slopes_a).reshape(t, A_WIDTH)
        mixed_b = _attn_b(proj3, slopes_b, sinks[i]).reshape(t, B_WIDTH)

        wr, br = _router_weights(w_grp[i], b_grp[i], w_exp[i], b_exp[i])
        h1, m, route, route_t, cnt = _outproj(h, mixed_a, mixed_b, w_out[i].astype(BF16),
                                              g_moe[i].reshape(1, d), wr, br)

        counts = cnt[0, :N_EXPERTS].astype(jnp.int32)
        offs = jnp.cumsum(counts) - counts
        eid = route_t[0:TOP_K].astype(jnp.int32)
        rank = route_t[TOP_K:2 * TOP_K].astype(jnp.int32)
        experts = jnp.arange(N_EXPERTS, dtype=jnp.int32)[:, None, None]
        seg_start = jnp.sum(jnp.where(eid[None] == experts, offs[:, None, None], 0), axis=0)
        pos_flat = (seg_start + rank).T.reshape(t * TOP_K)

        ys = _experts(_expert_work_items(counts, t * TOP_K), _sorted_tokens(pos_flat), m,
                      w_gate[i], w_up[i], w_down[i])
        h_next = _final(pos_flat, h1, route, p[i].reshape(t, PLE_DIM), w_ple[i].astype(BF16),
                        w_ple_gate[i].astype(BF16), g_ple[i].reshape(1, d), g_final.reshape(1, d), ys)
        h = h_next
    return h.reshape(b, s, d)
```

```python
import numpy as np
import jax
import jax.numpy as jnp
from jax import lax
from jax.experimental import pallas as pl
from jax.experimental.pallas import tpu as pltpu

D_MODEL = 2048
PLE_DIM = 256
BLK = 128
EPS = 1e-6
A_HEAD_DIM = 128
A_WIDTH = D_MODEL // 2
A_HEADS = A_WIDTH // A_HEAD_DIM
A_BRANCHES = ((128, 1), (512, 4), (2048, 16))
B_HEAD_DIM = 64
B_WIDTH = D_MODEL - A_WIDTH
B_HEADS = B_WIDTH // B_HEAD_DIM
B_GROUP = 8
B_KV_HEADS = B_HEADS // B_GROUP
B_WINDOW = 128
IN_COLS = 3 * A_WIDTH + B_WIDTH + 2 * B_KV_HEADS * B_HEAD_DIM
N_GROUPS = 4
EXPERTS_PER_GROUP = 8
N_EXPERTS = N_GROUPS * EXPERTS_PER_GROUP
TOP_K = 2
D_EXPERT = D_MODEL // 8

LANES = 128
SUBLANES = 8
V7X_VMEM_LIMIT_BYTES = 56 * 1024 * 1024

ROW_TILE = 512
EXPERT_TILE = 256
DMA_UNROLL = 16
DMA_PRIORITIES = 2
OUTPROJ_SPLIT = 2
F32 = jnp.float32
BF16 = jnp.bfloat16
NEG_INF = float("-inf")
LOG2E = 1.4426950408889634


def _cparams(*sem):
    return pltpu.CompilerParams(dimension_semantics=sem, vmem_limit_bytes=V7X_VMEM_LIMIT_BYTES)


def _rms(x, g):
    return x * lax.rsqrt(jnp.mean(x * x, axis=-1, keepdims=True) + EPS) * g


def _alibi_slopes(n):
    return jnp.asarray(np.array([2.0 ** (-8.0 * (i + 1) / n) for i in range(n)], dtype=np.float32))


def _proj_kernel(x_ref, g_ref, w_ref, o_ref):
    a = _rms(x_ref[...], g_ref[...]).astype(BF16)
    o_ref[...] = jnp.dot(a, w_ref[...], preferred_element_type=F32)


def _proj(x2, g, w_bf16):
    t, d = x2.shape
    n = w_bf16.shape[1]
    return pl.pallas_call(
        _proj_kernel,
        out_shape=jax.ShapeDtypeStruct((t, n), F32),
        grid=(t // ROW_TILE,),
        in_specs=[
            pl.BlockSpec((ROW_TILE, d), lambda i: (i, 0)),
            pl.BlockSpec((1, d), lambda i: (0, 0)),
            pl.BlockSpec((d, n), lambda i: (0, 0), pipeline_mode=pl.Buffered(1)),
        ],
        out_specs=pl.BlockSpec((ROW_TILE, n), lambda i: (i, 0)),
        compiler_params=_cparams("arbitrary"),
        name="proj",
    )(x2, g, w_bf16)


def _software_pipeline(items, stages):
    state = [None] * len(items)
    for t in range(len(items) + len(stages) - 1):
        for s, stage in enumerate(stages):
            j = t - s
            if 0 <= j < len(items):
                state[j] = stage(items[j], state[j])


def _nt_dot(a, b):
    return lax.dot_general(a, b, (((1,), (1,)), ((), ())), preferred_element_type=F32)


def _attn_a_kernel(slopes_ref, q_ref, k_ref, v_ref, o_ref, acc_ref, m_ref, den_ref):
    seq = q_ref.shape[1]
    slope = slopes_ref[pl.program_id(1)]
    scale2 = A_HEAD_DIM ** -0.5 * LOG2E
    qi = lax.broadcasted_iota(jnp.int32, (BLK, BLK), 0)
    kj = lax.broadcasted_iota(jnp.int32, (BLK, BLK), 1)
    d_cur = qi - kj
    d_prev = d_cur + BLK

    items = []
    for br, (window, dil) in enumerate(A_BRANCHES):
        max_delta = window // dil
        assert max_delta <= BLK and seq % (dil * BLK) == 0
        bias_cur = jnp.where((d_cur >= 0) & (d_cur <= max_delta),
                             -(slope * (dil * d_cur).astype(F32)) * LOG2E, NEG_INF)
        bias_prev = jnp.where((d_prev >= 0) & (d_prev <= max_delta),
                              -(slope * (dil * d_prev).astype(F32)) * LOG2E, NEG_INF)
        bias_both = jnp.concatenate([bias_prev, bias_cur], axis=1)
        for bi in range(seq // (dil * BLK)):
            for r in range(dil):
                items.append(dict(br=br, dil=dil, start=bi * dil * BLK + r, has_prev=bi > 0,
                                  bias=bias_both if bi > 0 else bias_cur))

    def rows(it, start):
        return pl.ds(start, BLK, stride=it["dil"]) if it["dil"] > 1 else pl.ds(start, BLK)

    def keys_or_values(ref, it):
        cur = ref[0, rows(it, it["start"]), :].astype(BF16)
        if not it["has_prev"]:
            return cur
        prev = ref[0, rows(it, it["start"] - it["dil"] * BLK), :].astype(BF16)
        return jnp.concatenate([prev, cur], axis=0)

    def scores(it, _):
        q = q_ref[0, rows(it, it["start"]), :].astype(BF16)
        return _nt_dot(q, keys_or_values(k_ref, it)) * scale2 + it["bias"]

    def softmax(it, s):
        m = jnp.max(s, axis=-1, keepdims=True)
        return m, jnp.exp2(s - m).astype(BF16)

    def values(it, m_p):
        m, p = m_p
        v = keys_or_values(v_ref, it)
        accden = jnp.dot(p, jnp.concatenate([v, jnp.ones_like(v)], axis=1), preferred_element_type=F32)
        r = rows(it, it["start"])
        acc_ref[it["br"], r, :] = accden[:, :LANES]
        den_ref[it["br"], r, :] = accden[:, LANES:]
        m_ref[it["br"], r, :] = jnp.broadcast_to(m, (BLK, LANES))

    _software_pipeline(items, (scores, softmax, values))

    chunk = 256
    def mix(ci, carry):
        r = pl.ds(pl.multiple_of(ci * chunk, chunk), chunk)
        m0, m1, m2 = m_ref[0, r, :], m_ref[1, r, :], m_ref[2, r, :]
        mx = jnp.maximum(jnp.maximum(m0, m1), m2)
        a0, a1, a2 = jnp.exp2(m0 - mx), jnp.exp2(m1 - mx), jnp.exp2(m2 - mx)
        num = a0 * acc_ref[0, r, :] + a1 * acc_ref[1, r, :] + a2 * acc_ref[2, r, :]
        tot = a0 * den_ref[0, r, :] + a1 * den_ref[1, r, :] + a2 * den_ref[2, r, :]
        o_ref[0, r, :] = (num / tot).astype(o_ref.dtype)
        return carry
    lax.fori_loop(0, seq // chunk, mix, 0)


def _attn_a(proj3, slopes):
    b, s, _ = proj3.shape
    nh = A_HEADS
    return pl.pallas_call(
        _attn_a_kernel,
        out_shape=jax.ShapeDtypeStruct((b, s, A_WIDTH), BF16),
        grid=(b, nh),
        in_specs=[
            pl.BlockSpec(memory_space=pltpu.SMEM),
            pl.BlockSpec((1, s, A_HEAD_DIM), lambda bi, h: (bi, 0, h)),
            pl.BlockSpec((1, s, A_HEAD_DIM), lambda bi, h: (bi, 0, nh + h)),
            pl.BlockSpec((1, s, A_HEAD_DIM), lambda bi, h: (bi, 0, 2 * nh + h)),
        ],
        out_specs=pl.BlockSpec((1, s, A_HEAD_DIM), lambda bi, h: (bi, 0, h)),
        scratch_shapes=[pltpu.VMEM((len(A_BRANCHES), s, LANES), F32)] * 3,
        compiler_params=_cparams("arbitrary", "arbitrary"),
        name="attn_a",
    )(slopes, proj3, proj3, proj3)


PAIRS_PER_KV = B_GROUP * B_HEAD_DIM // LANES


def _attn_b_kernel(slopes_ref, sinks_ref, q_ref, k_ref, v_ref, o_ref, k2_ref, v2_ref):
    seq = q_ref.shape[1]
    n_blk = seq // BLK
    pair = pl.program_id(1)
    kv_is0 = (pair // PAIRS_PER_KV) == 0
    scale = B_HEAD_DIM ** -0.5
    max_delta = B_WINDOW - 1
    lane = lax.broadcasted_iota(jnp.int32, (BLK, LANES), 1)
    lo_half = lane < B_HEAD_DIM

    row2 = lax.broadcasted_iota(jnp.int32, (2 * BLK, LANES), 0)
    lane2 = lax.broadcasted_iota(jnp.int32, (2 * BLK, LANES), 1)
    head_ones = jnp.where((row2 < BLK) == (lane2 < B_HEAD_DIM), 1.0, 0.0).astype(BF16)

    @pl.when(pair % PAIRS_PER_KV == 0)
    def _():
        def stage(bi, carry):
            r = pl.ds(pl.multiple_of(bi * BLK, BLK), BLK)
            r2 = pl.ds(pl.multiple_of(bi * 2 * BLK, 2 * BLK), 2 * BLK)
            staged = []
            for src in (k_ref, v_ref):
                t = src[0, r, :]
                t_rot = pltpu.roll(t, B_HEAD_DIM, axis=1)
                in_lo = jnp.where(kv_is0, t, t_rot)
                in_hi = jnp.where(kv_is0, t_rot, t)
                staged.append(jnp.concatenate([jnp.where(lo_half, in_lo, 0.0),
                                               jnp.where(lo_half, 0.0, in_hi)], axis=0).astype(BF16))
            k2_ref[r2, :] = staged[0]
            v2_ref[r2, :] = jnp.concatenate([staged[1], head_ones], axis=1)
            return carry
        lax.fori_loop(0, n_blk, stage, 0)

    qi = lax.broadcasted_iota(jnp.int32, (BLK, BLK), 0)
    kj = lax.broadcasted_iota(jnp.int32, (BLK, BLK), 1)
    d_cur = qi - kj
    d_prev = d_cur + BLK
    assert max_delta == BLK - 1
    in_prev = kj > qi
    heads = (2 * pair, 2 * pair + 1)
    bias_fold, bias_first, sink = [], [], []
    for hq in heads:
        slope = slopes_ref[hq]
        b_cur = -(slope * d_cur.astype(F32))
        bias_fold.append(jnp.where(in_prev, -(slope * d_prev.astype(F32)), b_cur))
        bias_first.append(jnp.where(in_prev, NEG_INF, b_cur))
        sink.append(sinks_ref[hq])

    def staged_rows(bi):
        first = max(bi - 1, 0)
        return pl.ds(first * 2 * BLK, (bi + 1 - first) * 2 * BLK)

    def scores(bi, _):
        q = (q_ref[0, pl.ds(bi * BLK, BLK), :] * scale).astype(BF16)
        return _nt_dot(q, k2_ref[staged_rows(bi), :])

    def softmax(bi, s_all):
        has_prev = bi > 0
        p_c, p_p, sink_term = [], [], []
        for j in range(2):
            if has_prev:
                s_p = s_all[:, j * BLK:(j + 1) * BLK]
                s_c = s_all[:, (2 + j) * BLK:(3 + j) * BLK]
                s = jnp.where(in_prev, s_p, s_c) + bias_fold[j]
            else:
                s = s_all[:, j * BLK:(j + 1) * BLK] + bias_first[j]
            m = jnp.maximum(jnp.max(s, axis=-1, keepdims=True), sink[j])
            p = jnp.exp(s - m)
            sink_term.append(jnp.exp(sink[j] - m))
            if has_prev:
                pp = jnp.where(in_prev, p, 0.0)
                p_p.append(pp.astype(BF16))
                p_c.append((p - pp).astype(BF16))
            else:
                p_c.append(p.astype(BF16))
        return jnp.concatenate(p_p + p_c, axis=1), jnp.where(lo_half, sink_term[0], sink_term[1])

    def values(bi, p_sink):
        p, sink_den = p_sink
        accden = jnp.dot(p, v2_ref[staged_rows(bi), :], preferred_element_type=F32)
        o_ref[0, pl.ds(bi * BLK, BLK), :] = (accden[:, :LANES] / (accden[:, LANES:] + sink_den)).astype(o_ref.dtype)

    _software_pipeline(list(range(n_blk)), (scores, softmax, values))


def _attn_b(proj3, slopes, sinks):
    b, s, _ = proj3.shape
    q0 = 3 * A_WIDTH // LANES
    kblk = (3 * A_WIDTH + B_WIDTH) // LANES
    assert B_KV_HEADS * B_HEAD_DIM == LANES
    return pl.pallas_call(
        _attn_b_kernel,
        out_shape=jax.ShapeDtypeStruct((b, s, B_WIDTH), BF16),
        grid=(b, B_WIDTH // LANES),
        in_specs=[
            pl.BlockSpec(memory_space=pltpu.SMEM),
            pl.BlockSpec(memory_space=pltpu.SMEM),
            pl.BlockSpec((1, s, LANES), lambda bi, p: (bi, 0, q0 + p)),
            pl.BlockSpec((1, s, LANES), lambda bi, p: (bi, 0, kblk)),
            pl.BlockSpec((1, s, LANES), lambda bi, p: (bi, 0, kblk + 1)),
        ],
        out_specs=pl.BlockSpec((1, s, LANES), lambda bi, p: (bi, 0, p)),
        scratch_shapes=[
            pltpu.VMEM((2 * s, LANES), BF16),
            pltpu.VMEM((2 * s, 2 * LANES), BF16),
        ],
        compiler_params=_cparams("arbitrary", "arbitrary"),
        name="attn_b",
    )(slopes, sinks, proj3, proj3, proj3)


GRP_LANE0 = N_EXPERTS


def _outproj_kernel(x_ref, ma_ref, mb_ref, wo_ref, g_ref, wr_ref, br_ref,
                    h1_ref, m_ref, route_ref, route_t_ref, cnt_ref, carry_ref):
    tm = x_ref.shape[0]
    half = tm // OUTPROJ_SPLIT

    @pl.when(pl.program_id(0) == 0)
    def _():
        carry_ref[...] = jnp.zeros_like(carry_ref)

    lane = lax.broadcasted_iota(jnp.int32, (half, LANES), 1).astype(F32)
    big = float(LANES)

    def rows(h):
        return pl.ds(h * half, half)

    def project(h, _):
        h1_ref[rows(h), :] = (x_ref[rows(h), :]
                              + jnp.dot(ma_ref[rows(h), :], wo_ref[0:A_WIDTH, :], preferred_element_type=F32)
                              + jnp.dot(mb_ref[rows(h), :], wo_ref[A_WIDTH:, :], preferred_element_type=F32))

    def normalise(h, _):
        m = _rms(h1_ref[rows(h), :], g_ref[...])
        m_ref[rows(h), :] = m
        return jnp.dot(m.astype(BF16), wr_ref[...], preferred_element_type=F32) + br_ref[...]

    def choose(h, logits):
        gl = jnp.where((lane >= GRP_LANE0) & (lane < GRP_LANE0 + N_GROUPS), logits, NEG_INF)
        gmax = jnp.max(gl, axis=-1, keepdims=True)
        gidx = jnp.min(jnp.where(gl == gmax, lane, big), axis=-1, keepdims=True) - GRP_LANE0
        grp_w = 1.0 / jnp.sum(jnp.exp(gl - gmax), axis=-1, keepdims=True)
        e_lo = gidx * EXPERTS_PER_GROUP
        el = jnp.where((lane >= e_lo) & (lane < e_lo + EXPERTS_PER_GROUP), logits, NEG_INF)
        v1 = jnp.max(el, axis=-1, keepdims=True)
        i1 = jnp.min(jnp.where(el == v1, lane, big), axis=-1, keepdims=True)
        el2 = jnp.where(lane == i1, NEG_INF, el)
        v2 = jnp.max(el2, axis=-1, keepdims=True)
        i2 = jnp.min(jnp.where(el2 == v2, lane, big), axis=-1, keepdims=True)
        e2 = jnp.exp(v2 - v1)
        return i1, i2, (1.0 / (1.0 + e2)) * grp_w, (e2 / (1.0 + e2)) * grp_w

    def rank(h, picks):
        i1, i2, w1, w2 = picks
        onehot = jnp.where(lane == i1, 1.0, 0.0) + jnp.where(lane == i2, 1.0, 0.0)
        row = lax.broadcasted_iota(jnp.int32, (half, half), 0)
        col = lax.broadcasted_iota(jnp.int32, (half, half), 1)
        earlier = jnp.where(row > col, 1.0, 0.0).astype(BF16)
        before = carry_ref[0:1, :] + jnp.dot(earlier, onehot.astype(BF16), preferred_element_type=F32)
        r1 = jnp.sum(jnp.where(lane == i1, before, 0.0), axis=-1, keepdims=True)
        r2 = jnp.sum(jnp.where(lane == i2, before, 0.0), axis=-1, keepdims=True)
        carry_ref[...] = carry_ref[...] + jnp.sum(onehot, axis=0, keepdims=True)
        route = jnp.zeros((half, LANES), F32)
        for idx, val in enumerate((i1, i2, r1, r2, w1, w2)):
            route = jnp.where(lane == float(idx), val, route)
        route_ref[rows(h), :] = route
        route_t_ref[:, rows(h)] = jnp.transpose(route)[0:SUBLANES, :]

    _software_pipeline(list(range(OUTPROJ_SPLIT)), (project, normalise, choose, rank))
    cnt_ref[...] = carry_ref[...]


def _outproj(x2, ma, mb, wo_bf16, g, wr_bf16, br):
    t, d = x2.shape
    tm = ROW_TILE
    row_spec = lambda w: pl.BlockSpec((tm, w), lambda i: (i, 0))
    const = lambda shape: pl.BlockSpec(shape, lambda i: (0, 0))
    return pl.pallas_call(
        _outproj_kernel,
        out_shape=(
            jax.ShapeDtypeStruct((t, d), F32),
            jax.ShapeDtypeStruct((t, d), F32),
            jax.ShapeDtypeStruct((t, LANES), F32),
            jax.ShapeDtypeStruct((SUBLANES, t), F32),
            jax.ShapeDtypeStruct((8, LANES), F32),
        ),
        grid=(t // tm,),
        in_specs=[
            row_spec(d), row_spec(A_WIDTH), row_spec(B_WIDTH),
            pl.BlockSpec((d, d), lambda i: (0, 0), pipeline_mode=pl.Buffered(1)),
            const((1, d)), const((d, LANES)), const((1, LANES)),
        ],
        out_specs=(row_spec(d), row_spec(d), row_spec(LANES), pl.BlockSpec((SUBLANES, tm), lambda i: (0, i)),
                   const((8, LANES))),
        scratch_shapes=[pltpu.VMEM((8, LANES), F32)],
        compiler_params=_cparams("arbitrary"),
        name="outproj",
    )(x2, ma, mb, wo_bf16, g, wr_bf16, br)


def _sorted_tokens_kernel(pos_ref, tok_ref):
    def body(t, c):
        for k in range(TOP_K):
            tok_ref[pos_ref[TOP_K * t + k]] = t
        return c
    lax.fori_loop(0, tok_ref.shape[0] // TOP_K, body, 0, unroll=DMA_UNROLL)


def _sorted_tokens(pos_flat):
    return pl.pallas_call(
        _sorted_tokens_kernel,
        out_shape=jax.ShapeDtypeStruct(pos_flat.shape, jnp.int32),
        grid_spec=pltpu.PrefetchScalarGridSpec(
            num_scalar_prefetch=1,
            grid=(1,),
            in_specs=[],
            out_specs=pl.BlockSpec(memory_space=pltpu.SMEM),
        ),
        compiler_params=_cparams("arbitrary"),
        name="sorted_tokens",
    )(pos_flat)


def _experts_kernel(tile_ref, exp_ref, lo_ref, hi_ref, first_ref, newexp_ref, slot_ref, tok_ref,
                    m_ref, wg_ref, wu_ref, wd_ref, y_ref, xbuf, wg_bf, wu_bf, wd_bf, sem):
    i = pl.program_id(0)
    n = pl.num_programs(0)
    tm, d = y_ref.shape
    groups = tm // SUBLANES
    lo, hi = lo_ref[i], hi_ref[i]

    def gather(item):
        base = tile_ref[item] * tm
        s = slot_ref[item]
        def body(g, c):
            for j in range(SUBLANES):
                tok = tok_ref[base + g * SUBLANES + j]
                pltpu.make_async_copy(m_ref.at[pl.ds(tok, 1)], xbuf.at[s, g, pl.ds(j, 1)],
                                      sem.at[s]).start(priority=j % DMA_PRIORITIES)
            return c
        lax.fori_loop(0, groups, body, 0, unroll=DMA_UNROLL // SUBLANES)

    @pl.when(i == 0)
    def _():
        gather(0)

    slot = slot_ref[i]

    @pl.when(first_ref[i] == 1)
    def _():
        for _ in range(tm):
            pltpu.make_async_copy(m_ref.at[pl.ds(0, 1)], xbuf.at[slot, 0, pl.ds(0, 1)], sem.at[slot]).wait()

    nxt = jnp.minimum(i + 1, n - 1)

    @pl.when((i + 1 < n) & (first_ref[nxt] == 1))
    def _():
        gather(nxt)

    @pl.when(newexp_ref[i] == 1)
    def _():
        wg_bf[...] = wg_ref[0].astype(BF16)
        wu_bf[...] = wu_ref[0].astype(BF16)
        wd_bf[...] = wd_ref[0].astype(BF16)

    @pl.when(hi > lo)
    def _():
        x = xbuf[slot].reshape(tm, d).astype(BF16)
        hg = jnp.dot(x, wg_bf[...], preferred_element_type=F32)
        hu = jnp.dot(x, wu_bf[...], preferred_element_type=F32)
        hid = (hg * jax.nn.sigmoid(hg)) * hu
        y = jnp.dot(hid.astype(BF16), wd_bf[...], preferred_element_type=F32)
        row = lax.broadcasted_iota(jnp.int32, (tm, 1), 0)
        mine = (row >= lo) & (row < hi)

        @pl.when(first_ref[i] == 1)
        def _():
            y_ref[...] = jnp.where(mine, y, 0.0)

        @pl.when(first_ref[i] == 0)
        def _():
            y_ref[...] = jnp.where(mine, y, y_ref[...])


def _experts(meta, tok_sorted, m, wg, wu, wd):
    d = m.shape[1]
    n = tok_sorted.shape[0]
    tm = EXPERT_TILE
    n_items = meta[0].shape[0]
    f = wg.shape[-1]
    prefetch = (*meta, tok_sorted)
    idx = lambda fn: (lambda i, tl, ex, *_: fn(i, tl, ex))
    return pl.pallas_call(
        _experts_kernel,
        out_shape=jax.ShapeDtypeStruct((n, d), F32),
        grid_spec=pltpu.PrefetchScalarGridSpec(
            num_scalar_prefetch=len(prefetch),
            grid=(n_items,),
            in_specs=[
                pl.BlockSpec(memory_space=pl.ANY),
                pl.BlockSpec((1, d, f), idx(lambda i, tl, ex: (ex[i], 0, 0))),
                pl.BlockSpec((1, d, f), idx(lambda i, tl, ex: (ex[i], 0, 0))),
                pl.BlockSpec((1, f, d), idx(lambda i, tl, ex: (ex[i], 0, 0))),
            ],
            out_specs=pl.BlockSpec((tm, d), idx(lambda i, tl, ex: (tl[i], 0))),
            scratch_shapes=[
                pltpu.VMEM((2, tm // SUBLANES, SUBLANES, d), F32),
                pltpu.VMEM((d, f), BF16), pltpu.VMEM((d, f), BF16), pltpu.VMEM((f, d), BF16),
                pltpu.SemaphoreType.DMA((2,)),
            ],
        ),
        compiler_params=_cparams("arbitrary"),
        name="experts",
    )(*prefetch, m, wg, wu, wd)


def _expert_work_items(counts, n_rows):
    tm = EXPERT_TILE
    n_tiles = n_rows // tm
    n_items = n_tiles + N_EXPERTS - 1
    offs = jnp.cumsum(counts) - counts
    ends = offs + counts
    t_first = offs // tm
    t_last = jnp.where(counts > 0, (ends - 1) // tm, t_first - 1)
    per_e = t_last - t_first + 1
    item_end = jnp.cumsum(per_e)
    item_start = item_end - per_e
    total = item_end[-1]
    ids = jnp.arange(n_items, dtype=jnp.int32)
    e = jnp.minimum(jnp.sum(item_end[None, :] <= ids[:, None], axis=1), N_EXPERTS - 1).astype(jnp.int32)
    tile = t_first[e] + (ids - item_start[e])
    lo = jnp.clip(offs[e] - tile * tm, 0, tm)
    hi = jnp.clip(ends[e] - tile * tm, 0, tm)
    valid = ids < total
    last = jnp.maximum(total - 1, 0)
    tile = jnp.where(valid, tile, tile[last])
    e = jnp.where(valid, e, e[last])
    lo = jnp.where(valid, lo, 0)
    hi = jnp.where(valid, hi, 0)
    prev_tile = jnp.concatenate([jnp.full((1,), -1, tile.dtype), tile[:-1]])
    first = (valid & (tile != prev_tile)).astype(jnp.int32)
    prev_e = jnp.concatenate([jnp.full((1,), -1, e.dtype), e[:-1]])
    new_expert = e != prev_e
    slot = (jnp.cumsum(first) + 1) % 2
    return tuple(a.astype(jnp.int32) for a in (tile, e, lo, hi, first, new_expert, slot))


def _final_kernel(pos_ref, h1_ref, route_ref, p_ref, wple_ref, wpg_ref, gple_ref, gfin_ref, ys_ref,
                  o_ref, ybuf_ref, h_ref, sem):
    tm = h1_ref.shape[0]
    groups = tm // SUBLANES
    i = pl.program_id(0)
    n = pl.num_programs(0)
    slot = i % 2

    def start_group(tile, slot_, g):
        base = tile * (tm * TOP_K)
        for j in range(SUBLANES):
            for k in range(TOP_K):
                src = pos_ref[base + TOP_K * (g * SUBLANES + j) + k]
                pltpu.make_async_copy(ys_ref.at[pl.ds(src, 1)], ybuf_ref.at[slot_, k, g, pl.ds(j, 1)],
                                      sem.at[slot_]).start(priority=k % DMA_PRIORITIES)

    def wait_tile(slot_):
        for _ in range(tm * TOP_K):
            pltpu.make_async_copy(ys_ref.at[pl.ds(0, 1)], ybuf_ref.at[slot_, 0, 0, pl.ds(0, 1)],
                                  sem.at[slot_]).wait()

    @pl.when(i == 0)
    def _():
        def body(g, c):
            start_group(0, 0, g)
            return c
        lax.fori_loop(0, groups, body, 0, unroll=DMA_UNROLL // SUBLANES)

    wait_tile(slot)

    lane = lax.broadcasted_iota(jnp.int32, (tm, LANES), 1)
    route = route_ref[...]
    w0 = jnp.sum(jnp.where(lane == 4, route, 0.0), axis=-1, keepdims=True)
    w1 = jnp.sum(jnp.where(lane == 5, route, 0.0), axis=-1, keepdims=True)
    d = ybuf_ref.shape[-1]
    moe = w0 * ybuf_ref[slot, 0].reshape(tm, d) + w1 * ybuf_ref[slot, 1].reshape(tm, d)
    h_ref[...] = h1_ref[...] + moe

    nxt = jnp.minimum(i + 1, n - 1)
    for g in range(groups):
        start_group(nxt, 1 - slot, g)

    h2 = h_ref[...]
    nrm = _rms(h2, gple_ref[...]).astype(BF16)
    gate = jax.nn.sigmoid(jnp.dot(nrm, wpg_ref[...], preferred_element_type=F32))
    pw = jnp.dot(p_ref[...].astype(BF16), wple_ref[...], preferred_element_type=F32)
    h3 = h2 + gate * pw
    o_ref[...] = _rms(h3, gfin_ref[...])

    @pl.when(i == n - 1)
    def _():
        wait_tile(1 - slot)


def _final(pos_flat, h1, route, p2, wple_bf16, wpg_bf16, gple, gfin, ys):
    t, d = h1.shape
    tm = ROW_TILE
    row_spec = lambda w: pl.BlockSpec((tm, w), lambda i, pos: (i, 0))
    const = lambda shape: pl.BlockSpec(shape, lambda i, pos: (0, 0))
    return pl.pallas_call(
        _final_kernel,
        out_shape=jax.ShapeDtypeStruct((t, d), F32),
        grid_spec=pltpu.PrefetchScalarGridSpec(
            num_scalar_prefetch=1,
            grid=(t // tm,),
            in_specs=[
                row_spec(d), row_spec(LANES), row_spec(PLE_DIM),
                const((PLE_DIM, d)),
                pl.BlockSpec((d, d), lambda i, pos: (0, 0), pipeline_mode=pl.Buffered(1)),
                const((1, d)), const((1, d)),
                pl.BlockSpec(memory_space=pl.ANY),
            ],
            out_specs=row_spec(d),
            scratch_shapes=[
                pltpu.VMEM((2, TOP_K, tm // SUBLANES, SUBLANES, ys.shape[1]), ys.dtype),
                pltpu.VMEM((tm, d), F32),
                pltpu.SemaphoreType.DMA((2,)),
            ],
        ),
        compiler_params=_cparams("arbitrary"),
        name="final",
    )(pos_flat, h1, route, p2, wple_bf16, wpg_bf16, gple, gfin, ys)


def _router_weights(w_grp, b_grp, w_exp, b_exp):
    d = w_grp.shape[0]
    pad = LANES - N_EXPERTS - N_GROUPS
    wr = jnp.concatenate([w_exp.reshape(d, N_EXPERTS), w_grp, jnp.zeros((d, pad), F32)], axis=1)
    br = jnp.concatenate([b_exp.reshape(N_EXPERTS), b_grp, jnp.zeros((pad,), F32)]).reshape(1, LANES)
    return wr.astype(BF16), br.astype(F32)


def kernel(x, p, w_in, w_out, sinks, g_mix, g_moe, g_ple, g_final, w_grp, b_grp, w_exp, b_exp,
           w_gate, w_up, w_down, w_ple, w_ple_gate):
    b, s, d = x.shape
    t = b * s
    depth = w_in.shape[0]
    assert d == D_MODEL and t % ROW_TILE == 0 and (t * TOP_K) % EXPERT_TILE == 0
    assert depth == 1, "the final rmsnorm is fused into the (single) layer's last kernel"
    slopes_a = _alibi_slopes(A_HEADS)
    slopes_b = _alibi_slopes(B_HEADS)

    h = x.reshape(t, d)
    for i in range(depth):
        proj = _proj(h, g_mix[i].reshape(1, d), w_in[i].astype(BF16))
        proj3 = proj.reshape(b, s, IN_COLS)
        mixed_a = _attn_a(proj3, slopes_a).reshape(t, A_WIDTH)
        mixed_b = _attn_b(proj3, slopes_b, sinks[i]).reshape(t, B_WIDTH)

        wr, br = _router_weights(w_grp[i], b_grp[i], w_exp[i], b_exp[i])
        h1, m, route, route_t, cnt = _outproj(h, mixed_a, mixed_b, w_out[i].astype(BF16),
                                              g_moe[i].reshape(1, d), wr, br)

        counts = cnt[0, :N_EXPERTS].astype(jnp.int32)
        offs = jnp.cumsum(counts) - counts
        eid = route_t[0:TOP_K].astype(jnp.int32)
        rank = route_t[TOP_K:2 * TOP_K].astype(jnp.int32)
        experts = jnp.arange(N_EXPERTS, dtype=jnp.int32)[:, None, None]
        seg_start = jnp.sum(jnp.where(eid[None] == experts, offs[:, None, None], 0), axis=0)
        pos_flat = (seg_start + rank).T.reshape(t * TOP_K)

        ys = _experts(_expert_work_items(counts, t * TOP_K), _sorted_tokens(pos_flat), m,
                      w_gate[i], w_up[i], w_down[i])
        h_next = _final(pos_flat, h1, route, p[i].reshape(t, PLE_DIM), w_ple[i].astype(BF16),
                        w_ple_gate[i].astype(BF16), g_ple[i].reshape(1, d), g_final.reshape(1, d), ys)
        h = h_next
    return h.reshape(b, s, d)
```

```python
import numpy as np
import jax
import jax.numpy as jnp
from jax import lax
from jax.experimental import pallas as pl
from jax.experimental.pallas import tpu as pltpu

D_MODEL = 2048
PLE_DIM = 256
BLK = 128
EPS = 1e-6
A_HEAD_DIM = 128
A_WIDTH = D_MODEL // 2
A_HEADS = A_WIDTH // A_HEAD_DIM
A_BRANCHES = ((128, 1), (512, 4), (2048, 16))
B_HEAD_DIM = 64
B_WIDTH = D_MODEL - A_WIDTH
B_HEADS = B_WIDTH // B_HEAD_DIM
B_GROUP = 8
B_KV_HEADS = B_HEADS // B_GROUP
B_WINDOW = 128
IN_COLS = 3 * A_WIDTH + B_WIDTH + 2 * B_KV_HEADS * B_HEAD_DIM
N_GROUPS = 4
EXPERTS_PER_GROUP = 8
N_EXPERTS = N_GROUPS * EXPERTS_PER_GROUP
TOP_K = 2
D_EXPERT = D_MODEL // 8

LANES = 128
SUBLANES = 8
V7X_VMEM_LIMIT_BYTES = 56 * 1024 * 1024

ROW_TILE = 512
EXPERT_TILE = 256
DMA_UNROLL = 16
DMA_PRIORITIES = 2
OUTPROJ_SPLIT = 2
WEIGHT_CHUNK = 128
F32 = jnp.float32
BF16 = jnp.bfloat16
NEG_INF = float("-inf")
LOG2E = 1.4426950408889634


def _cparams(*sem):
    return pltpu.CompilerParams(dimension_semantics=sem, vmem_limit_bytes=V7X_VMEM_LIMIT_BYTES)


def _rms(x, g):
    return x * lax.rsqrt(jnp.mean(x * x, axis=-1, keepdims=True) + EPS) * g


def _stage_bf16_weight(w_hbm, w_bf, stage, sem):
    chunk = stage.shape[1]
    n_chunks = w_hbm.shape[0] // chunk

    def copy(c):
        return pltpu.make_async_copy(w_hbm.at[pl.ds(c * chunk, chunk)], stage.at[c % 2], sem.at[c % 2])

    copy(0).start()
    for c in range(n_chunks):
        if c + 1 < n_chunks:
            copy(c + 1).start()
        copy(c).wait()
        w_bf[pl.ds(c * chunk, chunk), :] = stage[c % 2].astype(BF16)


def _weight_scratch(k, n):
    return [pltpu.VMEM((k, n), BF16), pltpu.VMEM((2, WEIGHT_CHUNK, n), F32), pltpu.SemaphoreType.DMA((2,))]


def _alibi_slopes(n):
    return jnp.asarray(np.array([2.0 ** (-8.0 * (i + 1) / n) for i in range(n)], dtype=np.float32))


def _proj_kernel(x_ref, g_ref, w_hbm, o_ref, w_bf, stage, sem):
    @pl.when(pl.program_id(0) == 0)
    def _():
        _stage_bf16_weight(w_hbm, w_bf, stage, sem)

    a = _rms(x_ref[...], g_ref[...]).astype(BF16)
    o_ref[...] = jnp.dot(a, w_bf[...], preferred_element_type=F32)


def _proj(x2, g, w):
    t, d = x2.shape
    n = w.shape[1]
    return pl.pallas_call(
        _proj_kernel,
        out_shape=jax.ShapeDtypeStruct((t, n), F32),
        grid=(t // ROW_TILE,),
        in_specs=[
            pl.BlockSpec((ROW_TILE, d), lambda i: (i, 0)),
            pl.BlockSpec((1, d), lambda i: (0, 0)),
            pl.BlockSpec(memory_space=pl.ANY),
        ],
        out_specs=pl.BlockSpec((ROW_TILE, n), lambda i: (i, 0)),
        scratch_shapes=_weight_scratch(d, n),
        compiler_params=_cparams("arbitrary"),
        name="proj",
    )(x2, g, w)


def _software_pipeline(items, stages):
    state = [None] * len(items)
    for t in range(len(items) + len(stages) - 1):
        for s, stage in enumerate(stages):
            j = t - s
            if 0 <= j < len(items):
                state[j] = stage(items[j], state[j])


def _nt_dot(a, b):
    return lax.dot_general(a, b, (((1,), (1,)), ((), ())), preferred_element_type=F32)


def _attn_a_kernel(slopes_ref, q_ref, k_ref, v_ref, o_ref, acc_ref, m_ref, den_ref):
    seq = q_ref.shape[1]
    slope = slopes_ref[pl.program_id(1)]
    scale2 = A_HEAD_DIM ** -0.5 * LOG2E
    qi = lax.broadcasted_iota(jnp.int32, (BLK, BLK), 0)
    kj = lax.broadcasted_iota(jnp.int32, (BLK, BLK), 1)
    d_cur = qi - kj
    d_prev = d_cur + BLK

    items = []
    for br, (window, dil) in enumerate(A_BRANCHES):
        max_delta = window // dil
        assert max_delta <= BLK and seq % (dil * BLK) == 0
        bias_cur = jnp.where((d_cur >= 0) & (d_cur <= max_delta),
                             -(slope * (dil * d_cur).astype(F32)) * LOG2E, NEG_INF)
        bias_prev = jnp.where((d_prev >= 0) & (d_prev <= max_delta),
                              -(slope * (dil * d_prev).astype(F32)) * LOG2E, NEG_INF)
        bias_both = jnp.concatenate([bias_prev, bias_cur], axis=1)
        for bi in range(seq // (dil * BLK)):
            for r in range(dil):
                items.append(dict(br=br, dil=dil, start=bi * dil * BLK + r, has_prev=bi > 0,
                                  bias=bias_both if bi > 0 else bias_cur))

    def rows(it, start):
        return pl.ds(start, BLK, stride=it["dil"]) if it["dil"] > 1 else pl.ds(start, BLK)

    def keys_or_values(ref, it):
        cur = ref[0, rows(it, it["start"]), :].astype(BF16)
        if not it["has_prev"]:
            return cur
        prev = ref[0, rows(it, it["start"] - it["dil"] * BLK), :].astype(BF16)
        return jnp.concatenate([prev, cur], axis=0)

    def scores(it, _):
        q = q_ref[0, rows(it, it["start"]), :].astype(BF16)
        return _nt_dot(q, keys_or_values(k_ref, it)) * scale2 + it["bias"]

    def softmax(it, s):
        m = jnp.max(s, axis=-1, keepdims=True)
        return m, jnp.exp2(s - m).astype(BF16)

    def values(it, m_p):
        m, p = m_p
        v = keys_or_values(v_ref, it)
        accden = jnp.dot(p, jnp.concatenate([v, jnp.ones_like(v)], axis=1), preferred_element_type=F32)
        r = rows(it, it["start"])
        acc_ref[it["br"], r, :] = accden[:, :LANES]
        den_ref[it["br"], r, :] = accden[:, LANES:]
        m_ref[it["br"], r, :] = jnp.broadcast_to(m, (BLK, LANES))

    _software_pipeline(items, (scores, softmax, values))

    chunk = 256
    def mix(ci, carry):
        r = pl.ds(pl.multiple_of(ci * chunk, chunk), chunk)
        m0, m1, m2 = m_ref[0, r, :], m_ref[1, r, :], m_ref[2, r, :]
        mx = jnp.maximum(jnp.maximum(m0, m1), m2)
        a0, a1, a2 = jnp.exp2(m0 - mx), jnp.exp2(m1 - mx), jnp.exp2(m2 - mx)
        num = a0 * acc_ref[0, r, :] + a1 * acc_ref[1, r, :] + a2 * acc_ref[2, r, :]
        tot = a0 * den_ref[0, r, :] + a1 * den_ref[1, r, :] + a2 * den_ref[2, r, :]
        o_ref[0, r, :] = (num / tot).astype(o_ref.dtype)
        return carry
    lax.fori_loop(0, seq // chunk, mix, 0)


def _attn_a(proj3, slopes):
    b, s, _ = proj3.shape
    nh = A_HEADS
    return pl.pallas_call(
        _attn_a_kernel,
        out_shape=jax.ShapeDtypeStruct((b, s, A_WIDTH), BF16),
        grid=(b, nh),
        in_specs=[
            pl.BlockSpec(memory_space=pltpu.SMEM),
            pl.BlockSpec((1, s, A_HEAD_DIM), lambda bi, h: (bi, 0, h)),
            pl.BlockSpec((1, s, A_HEAD_DIM), lambda bi, h: (bi, 0, nh + h)),
            pl.BlockSpec((1, s, A_HEAD_DIM), lambda bi, h: (bi, 0, 2 * nh + h)),
        ],
        out_specs=pl.BlockSpec((1, s, A_HEAD_DIM), lambda bi, h: (bi, 0, h)),
        scratch_shapes=[pltpu.VMEM((len(A_BRANCHES), s, LANES), F32)] * 3,
        compiler_params=_cparams("arbitrary", "arbitrary"),
        name="attn_a",
    )(slopes, proj3, proj3, proj3)


PAIRS_PER_KV = B_GROUP * B_HEAD_DIM // LANES


def _attn_b_kernel(slopes_ref, sinks_ref, q_ref, k_ref, v_ref, o_ref, k2_ref, v2_ref):
    seq = q_ref.shape[1]
    n_blk = seq // BLK
    pair = pl.program_id(1)
    kv_is0 = (pair // PAIRS_PER_KV) == 0
    scale = B_HEAD_DIM ** -0.5
    max_delta = B_WINDOW - 1
    lane = lax.broadcasted_iota(jnp.int32, (BLK, LANES), 1)
    lo_half = lane < B_HEAD_DIM

    row2 = lax.broadcasted_iota(jnp.int32, (2 * BLK, LANES), 0)
    lane2 = lax.broadcasted_iota(jnp.int32, (2 * BLK, LANES), 1)
    head_ones = jnp.where((row2 < BLK) == (lane2 < B_HEAD_DIM), 1.0, 0.0).astype(BF16)

    @pl.when(pair % PAIRS_PER_KV == 0)
    def _():
        def stage(bi, carry):
            r = pl.ds(pl.multiple_of(bi * BLK, BLK), BLK)
            r2 = pl.ds(pl.multiple_of(bi * 2 * BLK, 2 * BLK), 2 * BLK)
            staged = []
            for src in (k_ref, v_ref):
                t = src[0, r, :]
                t_rot = pltpu.roll(t, B_HEAD_DIM, axis=1)
                in_lo = jnp.where(kv_is0, t, t_rot)
                in_hi = jnp.where(kv_is0, t_rot, t)
                staged.append(jnp.concatenate([jnp.where(lo_half, in_lo, 0.0),
                                               jnp.where(lo_half, 0.0, in_hi)], axis=0).astype(BF16))
            k2_ref[r2, :] = staged[0]
            v2_ref[r2, :] = jnp.concatenate([staged[1], head_ones], axis=1)
            return carry
        lax.fori_loop(0, n_blk, stage, 0)

    qi = lax.broadcasted_iota(jnp.int32, (BLK, BLK), 0)
    kj = lax.broadcasted_iota(jnp.int32, (BLK, BLK), 1)
    d_cur = qi - kj
    d_prev = d_cur + BLK
    assert max_delta == BLK - 1
    in_prev = kj > qi
    heads = (2 * pair, 2 * pair + 1)
    bias_fold, bias_first, sink = [], [], []
    for hq in heads:
        slope = slopes_ref[hq]
        b_cur = -(slope * d_cur.astype(F32))
        bias_fold.append(jnp.where(in_prev, -(slope * d_prev.astype(F32)), b_cur))
        bias_first.append(jnp.where(in_prev, NEG_INF, b_cur))
        sink.append(sinks_ref[hq])

    def staged_rows(bi):
        first = max(bi - 1, 0)
        return pl.ds(first * 2 * BLK, (bi + 1 - first) * 2 * BLK)

    def scores(bi, _):
        q = (q_ref[0, pl.ds(bi * BLK, BLK), :] * scale).astype(BF16)
        return _nt_dot(q, k2_ref[staged_rows(bi), :])

    def softmax(bi, s_all):
        has_prev = bi > 0
        p_c, p_p, sink_term = [], [], []
        for j in range(2):
            if has_prev:
                s_p = s_all[:, j * BLK:(j + 1) * BLK]
                s_c = s_all[:, (2 + j) * BLK:(3 + j) * BLK]
                s = jnp.where(in_prev, s_p, s_c) + bias_fold[j]
            else:
                s = s_all[:, j * BLK:(j + 1) * BLK] + bias_first[j]
            m = jnp.maximum(jnp.max(s, axis=-1, keepdims=True), sink[j])
            p = jnp.exp(s - m)
            sink_term.append(jnp.exp(sink[j] - m))
            if has_prev:
                pp = jnp.where(in_prev, p, 0.0)
                p_p.append(pp.astype(BF16))
                p_c.append((p - pp).astype(BF16))
            else:
                p_c.append(p.astype(BF16))
        return jnp.concatenate(p_p + p_c, axis=1), jnp.where(lo_half, sink_term[0], sink_term[1])

    def values(bi, p_sink):
        p, sink_den = p_sink
        accden = jnp.dot(p, v2_ref[staged_rows(bi), :], preferred_element_type=F32)
        o_ref[0, pl.ds(bi * BLK, BLK), :] = (accden[:, :LANES] / (accden[:, LANES:] + sink_den)).astype(o_ref.dtype)

    _software_pipeline(list(range(n_blk)), (scores, softmax, values))


def _attn_b(proj3, slopes, sinks):
    b, s, _ = proj3.shape
    q0 = 3 * A_WIDTH // LANES
    kblk = (3 * A_WIDTH + B_WIDTH) // LANES
    assert B_KV_HEADS * B_HEAD_DIM == LANES
    return pl.pallas_call(
        _attn_b_kernel,
        out_shape=jax.ShapeDtypeStruct((b, s, B_WIDTH), BF16),
        grid=(b, B_WIDTH // LANES),
        in_specs=[
            pl.BlockSpec(memory_space=pltpu.SMEM),
            pl.BlockSpec(memory_space=pltpu.SMEM),
            pl.BlockSpec((1, s, LANES), lambda bi, p: (bi, 0, q0 + p)),
            pl.BlockSpec((1, s, LANES), lambda bi, p: (bi, 0, kblk)),
            pl.BlockSpec((1, s, LANES), lambda bi, p: (bi, 0, kblk + 1)),
        ],
        out_specs=pl.BlockSpec((1, s, LANES), lambda bi, p: (bi, 0, p)),
        scratch_shapes=[
            pltpu.VMEM((2 * s, LANES), BF16),
            pltpu.VMEM((2 * s, 2 * LANES), BF16),
        ],
        compiler_params=_cparams("arbitrary", "arbitrary"),
        name="attn_b",
    )(slopes, sinks, proj3, proj3, proj3)


GRP_LANE0 = N_EXPERTS


def _outproj_kernel(x_ref, ma_ref, mb_ref, wo_hbm, g_ref, wr_ref, br_ref,
                    h1_ref, m_ref, route_ref, route_t_ref, cnt_ref, carry_ref, wo_ref, stage, sem):
    tm = x_ref.shape[0]
    half = tm // OUTPROJ_SPLIT

    @pl.when(pl.program_id(0) == 0)
    def _():
        carry_ref[...] = jnp.zeros_like(carry_ref)
        _stage_bf16_weight(wo_hbm, wo_ref, stage, sem)

    lane = lax.broadcasted_iota(jnp.int32, (half, LANES), 1).astype(F32)
    big = float(LANES)

    def rows(h):
        return pl.ds(h * half, half)

    def project(h, _):
        h1_ref[rows(h), :] = (x_ref[rows(h), :]
                              + jnp.dot(ma_ref[rows(h), :], wo_ref[0:A_WIDTH, :], preferred_element_type=F32)
                              + jnp.dot(mb_ref[rows(h), :], wo_ref[A_WIDTH:, :], preferred_element_type=F32))

    def normalise(h, _):
        m = _rms(h1_ref[rows(h), :], g_ref[...])
        m_ref[rows(h), :] = m
        return jnp.dot(m.astype(BF16), wr_ref[...], preferred_element_type=F32) + br_ref[...]

    def choose(h, logits):
        gl = jnp.where((lane >= GRP_LANE0) & (lane < GRP_LANE0 + N_GROUPS), logits, NEG_INF)
        gmax = jnp.max(gl, axis=-1, keepdims=True)
        gidx = jnp.min(jnp.where(gl == gmax, lane, big), axis=-1, keepdims=True) - GRP_LANE0
        grp_w = 1.0 / jnp.sum(jnp.exp(gl - gmax), axis=-1, keepdims=True)
        e_lo = gidx * EXPERTS_PER_GROUP
        el = jnp.where((lane >= e_lo) & (lane < e_lo + EXPERTS_PER_GROUP), logits, NEG_INF)
        v1 = jnp.max(el, axis=-1, keepdims=True)
        i1 = jnp.min(jnp.where(el == v1, lane, big), axis=-1, keepdims=True)
        el2 = jnp.where(lane == i1, NEG_INF, el)
        v2 = jnp.max(el2, axis=-1, keepdims=True)
        i2 = jnp.min(jnp.where(el2 == v2, lane, big), axis=-1, keepdims=True)
        e2 = jnp.exp(v2 - v1)
        return i1, i2, (1.0 / (1.0 + e2)) * grp_w, (e2 / (1.0 + e2)) * grp_w

    def rank(h, picks):
        i1, i2, w1, w2 = picks
        onehot = jnp.where(lane == i1, 1.0, 0.0) + jnp.where(lane == i2, 1.0, 0.0)
        row = lax.broadcasted_iota(jnp.int32, (half, half), 0)
        col = lax.broadcasted_iota(jnp.int32, (half, half), 1)
        earlier = jnp.where(row > col, 1.0, 0.0).astype(BF16)
        before = carry_ref[0:1, :] + jnp.dot(earlier, onehot.astype(BF16), preferred_element_type=F32)
        r1 = jnp.sum(jnp.where(lane == i1, before, 0.0), axis=-1, keepdims=True)
        r2 = jnp.sum(jnp.where(lane == i2, before, 0.0), axis=-1, keepdims=True)
        carry_ref[...] = carry_ref[...] + jnp.sum(onehot, axis=0, keepdims=True)
        route = jnp.zeros((half, LANES), F32)
        for idx, val in enumerate((i1, i2, r1, r2, w1, w2)):
            route = jnp.where(lane == float(idx), val, route)
        route_ref[rows(h), :] = route
        route_t_ref[:, rows(h)] = jnp.transpose(route)[0:SUBLANES, :]

    _software_pipeline(list(range(OUTPROJ_SPLIT)), (project, normalise, choose, rank))
    cnt_ref[...] = carry_ref[...]


def _outproj(x2, ma, mb, wo, g, wr_bf16, br):
    t, d = x2.shape
    tm = ROW_TILE
    row_spec = lambda w: pl.BlockSpec((tm, w), lambda i: (i, 0))
    const = lambda shape: pl.BlockSpec(shape, lambda i: (0, 0))
    return pl.pallas_call(
        _outproj_kernel,
        out_shape=(
            jax.ShapeDtypeStruct((t, d), F32),
            jax.ShapeDtypeStruct((t, d), F32),
            jax.ShapeDtypeStruct((t, LANES), F32),
            jax.ShapeDtypeStruct((SUBLANES, t), F32),
            jax.ShapeDtypeStruct((8, LANES), F32),
        ),
        grid=(t // tm,),
        in_specs=[
            row_spec(d), row_spec(A_WIDTH), row_spec(B_WIDTH),
            pl.BlockSpec(memory_space=pl.ANY),
            const((1, d)), const((d, LANES)), const((1, LANES)),
        ],
        out_specs=(row_spec(d), row_spec(d), row_spec(LANES), pl.BlockSpec((SUBLANES, tm), lambda i: (0, i)),
                   const((8, LANES))),
        scratch_shapes=[pltpu.VMEM((8, LANES), F32)] + _weight_scratch(d, d),
        compiler_params=_cparams("arbitrary"),
        name="outproj",
    )(x2, ma, mb, wo, g, wr_bf16, br)


def _sorted_tokens_kernel(pos_ref, tok_ref):
    n_tok = tok_ref.shape[0] // TOP_K

    def body(t, c):
        for k in range(TOP_K):
            tok_ref[pos_ref[k * n_tok + t]] = t
        return c
    lax.fori_loop(0, n_tok, body, 0, unroll=DMA_UNROLL)


def _sorted_tokens(pos_flat):
    return pl.pallas_call(
        _sorted_tokens_kernel,
        out_shape=jax.ShapeDtypeStruct(pos_flat.shape, jnp.int32),
        grid_spec=pltpu.PrefetchScalarGridSpec(
            num_scalar_prefetch=1,
            grid=(1,),
            in_specs=[],
            out_specs=pl.BlockSpec(memory_space=pltpu.SMEM),
        ),
        compiler_params=_cparams("arbitrary"),
        name="sorted_tokens",
    )(pos_flat)


def _experts_kernel(tile_ref, exp_ref, lo_ref, hi_ref, first_ref, newexp_ref, slot_ref, tok_ref,
                    m_ref, wg_ref, wu_ref, wd_ref, y_ref, xbuf, wg_bf, wu_bf, wd_bf, sem):
    i = pl.program_id(0)
    n = pl.num_programs(0)
    tm, d = y_ref.shape
    groups = tm // SUBLANES
    lo, hi = lo_ref[i], hi_ref[i]

    def gather(item):
        base = tile_ref[item] * tm
        s = slot_ref[item]
        def body(g, c):
            for j in range(SUBLANES):
                tok = tok_ref[base + g * SUBLANES + j]
                pltpu.make_async_copy(m_ref.at[pl.ds(tok, 1)], xbuf.at[s, g, pl.ds(j, 1)],
                                      sem.at[s]).start(priority=j % DMA_PRIORITIES)
            return c
        lax.fori_loop(0, groups, body, 0, unroll=DMA_UNROLL // SUBLANES)

    @pl.when(i == 0)
    def _():
        gather(0)

    slot = slot_ref[i]

    @pl.when(first_ref[i] == 1)
    def _():
        for _ in range(tm):
            pltpu.make_async_copy(m_ref.at[pl.ds(0, 1)], xbuf.at[slot, 0, pl.ds(0, 1)], sem.at[slot]).wait()

    nxt = jnp.minimum(i + 1, n - 1)

    @pl.when((i + 1 < n) & (first_ref[nxt] == 1))
    def _():
        gather(nxt)

    @pl.when(newexp_ref[i] == 1)
    def _():
        wg_bf[...] = wg_ref[0].astype(BF16)
        wu_bf[...] = wu_ref[0].astype(BF16)
        wd_bf[...] = wd_ref[0].astype(BF16)

    @pl.when(hi > lo)
    def _():
        x = xbuf[slot].reshape(tm, d).astype(BF16)
        hg = jnp.dot(x, wg_bf[...], preferred_element_type=F32)
        hu = jnp.dot(x, wu_bf[...], preferred_element_type=F32)
        hid = (hg * jax.nn.sigmoid(hg)) * hu
        y = jnp.dot(hid.astype(BF16), wd_bf[...], preferred_element_type=F32)
        row = lax.broadcasted_iota(jnp.int32, (tm, 1), 0)
        mine = (row >= lo) & (row < hi)

        @pl.when(first_ref[i] == 1)
        def _():
            y_ref[...] = jnp.where(mine, y, 0.0)

        @pl.when(first_ref[i] == 0)
        def _():
            y_ref[...] = jnp.where(mine, y, y_ref[...])


def _experts(meta, tok_sorted, m, wg, wu, wd):
    d = m.shape[1]
    n = tok_sorted.shape[0]
    tm = EXPERT_TILE
    n_items = meta[0].shape[0]
    f = wg.shape[-1]
    prefetch = (*meta, tok_sorted)
    idx = lambda fn: (lambda i, tl, ex, *_: fn(i, tl, ex))
    return pl.pallas_call(
        _experts_kernel,
        out_shape=jax.ShapeDtypeStruct((n, d), F32),
        grid_spec=pltpu.PrefetchScalarGridSpec(
            num_scalar_prefetch=len(prefetch),
            grid=(n_items,),
            in_specs=[
                pl.BlockSpec(memory_space=pl.ANY),
                pl.BlockSpec((1, d, f), idx(lambda i, tl, ex: (ex[i], 0, 0))),
                pl.BlockSpec((1, d, f), idx(lambda i, tl, ex: (ex[i], 0, 0))),
                pl.BlockSpec((1, f, d), idx(lambda i, tl, ex: (ex[i], 0, 0))),
            ],
            out_specs=pl.BlockSpec((tm, d), idx(lambda i, tl, ex: (tl[i], 0))),
            scratch_shapes=[
                pltpu.VMEM((2, tm // SUBLANES, SUBLANES, d), F32),
                pltpu.VMEM((d, f), BF16), pltpu.VMEM((d, f), BF16), pltpu.VMEM((f, d), BF16),
                pltpu.SemaphoreType.DMA((2,)),
            ],
        ),
        compiler_params=_cparams("arbitrary"),
        name="experts",
    )(*prefetch, m, wg, wu, wd)


def _expert_work_items(counts, n_rows):
    tm = EXPERT_TILE
    n_tiles = n_rows // tm
    n_items = n_tiles + N_EXPERTS - 1
    offs = jnp.cumsum(counts) - counts
    ends = offs + counts
    t_first = offs // tm
    t_last = jnp.where(counts > 0, (ends - 1) // tm, t_first - 1)
    per_e = t_last - t_first + 1
    item_end = jnp.cumsum(per_e)
    item_start = item_end - per_e
    total = item_end[-1]
    ids = jnp.arange(n_items, dtype=jnp.int32)
    e = jnp.minimum(jnp.sum(item_end[None, :] <= ids[:, None], axis=1), N_EXPERTS - 1).astype(jnp.int32)
    tile = t_first[e] + (ids - item_start[e])
    lo = jnp.clip(offs[e] - tile * tm, 0, tm)
    hi = jnp.clip(ends[e] - tile * tm, 0, tm)
    valid = ids < total
    last = jnp.maximum(total - 1, 0)
    tile = jnp.where(valid, tile, tile[last])
    e = jnp.where(valid, e, e[last])
    lo = jnp.where(valid, lo, 0)
    hi = jnp.where(valid, hi, 0)
    prev_tile = jnp.concatenate([jnp.full((1,), -1, tile.dtype), tile[:-1]])
    first = (valid & (tile != prev_tile)).astype(jnp.int32)
    prev_e = jnp.concatenate([jnp.full((1,), -1, e.dtype), e[:-1]])
    new_expert = e != prev_e
    slot = (jnp.cumsum(first) + 1) % 2
    return tuple(a.astype(jnp.int32) for a in (tile, e, lo, hi, first, new_expert, slot))


def _final_kernel(pos_ref, h1_ref, route_ref, p_ref, wple_ref, wpg_hbm, gple_ref, gfin_ref, ys_ref,
                  o_ref, ybuf_ref, h_ref, sem, wpg_ref, stage, wsem):
    tm = h1_ref.shape[0]
    groups = tm // SUBLANES
    i = pl.program_id(0)
    n = pl.num_programs(0)
    slot = i % 2

    n_tok = pos_ref.shape[0] // TOP_K

    def start_group(tile, slot_, g):
        base = tile * tm
        for j in range(SUBLANES):
            for k in range(TOP_K):
                src = pos_ref[k * n_tok + base + g * SUBLANES + j]
                pltpu.make_async_copy(ys_ref.at[pl.ds(src, 1)], ybuf_ref.at[slot_, k, g, pl.ds(j, 1)],
                                      sem.at[slot_]).start(priority=k % DMA_PRIORITIES)

    def wait_tile(slot_):
        for _ in range(tm * TOP_K):
            pltpu.make_async_copy(ys_ref.at[pl.ds(0, 1)], ybuf_ref.at[slot_, 0, 0, pl.ds(0, 1)],
                                  sem.at[slot_]).wait()

    @pl.when(i == 0)
    def _():
        def body(g, c):
            start_group(0, 0, g)
            return c
        lax.fori_loop(0, groups, body, 0, unroll=DMA_UNROLL // SUBLANES)
        _stage_bf16_weight(wpg_hbm, wpg_ref, stage, wsem)

    wait_tile(slot)

    lane = lax.broadcasted_iota(jnp.int32, (tm, LANES), 1)
    route = route_ref[...]
    w0 = jnp.sum(jnp.where(lane == 4, route, 0.0), axis=-1, keepdims=True)
    w1 = jnp.sum(jnp.where(lane == 5, route, 0.0), axis=-1, keepdims=True)
    d = ybuf_ref.shape[-1]
    moe = w0 * ybuf_ref[slot, 0].reshape(tm, d) + w1 * ybuf_ref[slot, 1].reshape(tm, d)
    h_ref[...] = h1_ref[...] + moe

    nxt = jnp.minimum(i + 1, n - 1)
    for g in range(groups):
        start_group(nxt, 1 - slot, g)

    h2 = h_ref[...]
    nrm = _rms(h2, gple_ref[...]).astype(BF16)
    gate = jax.nn.sigmoid(jnp.dot(nrm, wpg_ref[...], preferred_element_type=F32))
    pw = jnp.dot(p_ref[...].astype(BF16), wple_ref[...], preferred_element_type=F32)
    h3 = h2 + gate * pw
    o_ref[...] = _rms(h3, gfin_ref[...])

    @pl.when(i == n - 1)
    def _():
        wait_tile(1 - slot)


def _final(pos_flat, h1, route, p2, wple_bf16, wpg, gple, gfin, ys):
    t, d = h1.shape
    tm = ROW_TILE
    row_spec = lambda w: pl.BlockSpec((tm, w), lambda i, pos: (i, 0))
    const = lambda shape: pl.BlockSpec(shape, lambda i, pos: (0, 0))
    return pl.pallas_call(
        _final_kernel,
        out_shape=jax.ShapeDtypeStruct((t, d), F32),
        grid_spec=pltpu.PrefetchScalarGridSpec(
            num_scalar_prefetch=1,
            grid=(t // tm,),
            in_specs=[
                row_spec(d), row_spec(LANES), row_spec(PLE_DIM),
                const((PLE_DIM, d)),
                pl.BlockSpec(memory_space=pl.ANY),
                const((1, d)), const((1, d)),
                pl.BlockSpec(memory_space=pl.ANY),
            ],
            out_specs=row_spec(d),
            scratch_shapes=[
                pltpu.VMEM((2, TOP_K, tm // SUBLANES, SUBLANES, ys.shape[1]), ys.dtype),
                pltpu.VMEM((tm, d), F32),
                pltpu.SemaphoreType.DMA((2,)),
            ] + _weight_scratch(d, d),
        ),
        compiler_params=_cparams("arbitrary"),
        name="final",
    )(pos_flat, h1, route, p2, wple_bf16, wpg, gple, gfin, ys)


def _router_weights(w_grp, b_grp, w_exp, b_exp):
    d = w_grp.shape[0]
    pad = LANES - N_EXPERTS - N_GROUPS
    wr = jnp.concatenate([w_exp.reshape(d, N_EXPERTS), w_grp, jnp.zeros((d, pad), F32)], axis=1)
    br = jnp.concatenate([b_exp.reshape(N_EXPERTS), b_grp, jnp.zeros((pad,), F32)]).reshape(1, LANES)
    return wr.astype(BF16), br.astype(F32)


def kernel(x, p, w_in, w_out, sinks, g_mix, g_moe, g_ple, g_final, w_grp, b_grp, w_exp, b_exp,
           w_gate, w_up, w_down, w_ple, w_ple_gate):
    b, s, d = x.shape
    t = b * s
    depth = w_in.shape[0]
    assert d == D_MODEL and t % ROW_TILE == 0 and (t * TOP_K) % EXPERT_TILE == 0
    assert depth == 1, "the final rmsnorm is fused into the (single) layer's last kernel"
    slopes_a = _alibi_slopes(A_HEADS)
    slopes_b = _alibi_slopes(B_HEADS)

    h = x.reshape(t, d)
    for i in range(depth):
        proj = _proj(h, g_mix[i].reshape(1, d), w_in[i])
        proj3 = proj.reshape(b, s, IN_COLS)
        mixed_a = _attn_a(proj3, slopes_a).reshape(t, A_WIDTH)
        mixed_b = _attn_b(proj3, slopes_b, sinks[i]).reshape(t, B_WIDTH)

        wr, br = _router_weights(w_grp[i], b_grp[i], w_exp[i], b_exp[i])
        h1, m, route, route_t, cnt = _outproj(h, mixed_a, mixed_b, w_out[i], g_moe[i].reshape(1, d), wr, br)

        counts = cnt[0, :N_EXPERTS].astype(jnp.int32)
        offs = jnp.cumsum(counts) - counts
        eid = route_t[0:TOP_K].astype(jnp.int32)
        rank = route_t[TOP_K:2 * TOP_K].astype(jnp.int32)
        experts = jnp.arange(N_EXPERTS, dtype=jnp.int32)[:, None, None]
        seg_start = jnp.sum(jnp.where(eid[None] == experts, offs[:, None, None], 0), axis=0)
        pos_flat = (seg_start + rank).reshape(TOP_K * t)

        ys = _experts(_expert_work_items(counts, t * TOP_K), _sorted_tokens(pos_flat), m,
                      w_gate[i], w_up[i], w_down[i])
        h_next = _final(pos_flat, h1, route, p[i].reshape(t, PLE_DIM), w_ple[i].astype(BF16),
                        w_ple_gate[i], g_ple[i].reshape(1, d), g_final.reshape(1, d), ys)
        h = h_next
    return h.reshape(b, s, d)
```
